```python
import math
import jax, jax.numpy as jnp
from jax import lax
import numpy as np

D_MODEL = 2048
BATCH = 2
SEQ = 16384
DEPTH = 2

PLE_DIM = 256
D_FF = 5632
EPS = 1e-6
ROPE_BASE = 10000.0

SSM_WIDTH = 512
SSM_GROUP = 16
SSM_GROUPS = SSM_WIDTH // SSM_GROUP
SSM_STATE = 64

MLSTM_HEADS = 4
MLSTM_QK = 64
MLSTM_V = 128
MLSTM_CHUNK = 128
MLSTM_CONV = 4

MLA_HEADS = 4
MLA_Q_RANK = 384
MLA_KV_RANK = 256
MLA_NOPE = 128
MLA_ROPE = 64
MLA_V = 128
MLA_QBLOCK = 128

RET_HEADS = 4
RET_QK = 64
RET_V = 128
RET_CHUNK = 128

MIX_SIZES = (SSM_WIDTH, MLSTM_HEADS * MLSTM_V, MLA_HEADS * MLA_V, RET_HEADS * RET_V)
D_MIX = sum(MIX_SIZES)
MIX_OFFSETS = tuple(int(v) for v in np.cumsum(MIX_SIZES)[:-1])

IN_SIZES = (SSM_WIDTH,
            MLSTM_HEADS * MLSTM_QK, MLSTM_HEADS * MLSTM_QK,
            MLSTM_HEADS * MLSTM_V, MLSTM_HEADS * MLSTM_V,
            MLSTM_HEADS, MLSTM_HEADS,
            MLA_Q_RANK, MLA_KV_RANK, MLA_ROPE,
            RET_HEADS * RET_QK, RET_HEADS * RET_QK,
            RET_HEADS * RET_V, RET_HEADS * RET_V)
D_IN = sum(IN_SIZES)
IN_OFFSETS = tuple(int(v) for v in np.cumsum(IN_SIZES)[:-1])

kernel_name = "hymba_style_four_mixer_hybrid"


def rms_norm(x, gain):
    xf = x.astype(jnp.float32)
    y = xf * lax.rsqrt(jnp.mean(xf * xf, axis=-1, keepdims=True) + EPS)
    return (y * gain.astype(jnp.float32)).astype(x.dtype)


def rope_angles(positions, dim):
    inv = 1.0 / (ROPE_BASE ** (jnp.arange(0, dim, 2, dtype=jnp.float32) / dim))
    ang = positions.astype(jnp.float32)[..., None] * inv
    return jnp.cos(ang), jnp.sin(ang)


def apply_rope(x, cos, sin):
    x1, x2 = jnp.split(x.astype(jnp.float32), 2, axis=-1)
    c = cos[:, :, None, :]
    s = sin[:, :, None, :]
    return jnp.concatenate([x1 * c - x2 * s, x1 * s + x2 * c], axis=-1)


def swiglu(x, w_gate, w_up, w_down):
    return (jax.nn.silu(x @ w_gate) * (x @ w_up)) @ w_down


def causal_depthwise_conv(x, w, b):
    k = w.shape[0]
    y = lax.conv_general_dilated(x, w[:, None, :].astype(x.dtype), window_strides=(1,),
                                 padding=[(k - 1, 0)], dimension_numbers=('NWC', 'WIO', 'NWC'),
                                 feature_group_count=x.shape[-1])
    return y + b


def s5_mixer(u, a_re, a_im, b_re, b_im, c_re, c_im, d_skip, log_dt, w_glu, b_glu):
    bsz, seq, _ = u.shape
    f32 = jnp.float32
    uf = u.astype(f32)
    ug = uf.reshape(bsz, seq, SSM_GROUPS, SSM_GROUP)
    lr, li = a_re.astype(f32), a_im.astype(f32)
    dt = jnp.exp(log_dt.astype(f32))[:, None]
    mag = jnp.exp(lr * dt)
    ar, ai = mag * jnp.cos(li * dt), mag * jnp.sin(li * dt)
    den = lr * lr + li * li
    nr, ni = ar - 1.0, ai
    zr = (nr * lr + ni * li) / den
    zi = (ni * lr - nr * li) / den
    br_, bi_ = b_re.astype(f32), b_im.astype(f32)
    bbr = zr[..., None] * br_ - zi[..., None] * bi_
    bbi = zr[..., None] * bi_ + zi[..., None] * br_
    xr = jnp.einsum('gnc,bsgc->bsgn', bbr, ug)
    xi = jnp.einsum('gnc,bsgc->bsgn', bbi, ug)
    a_r = jnp.broadcast_to(ar, (1, seq) + ar.shape)
    a_i = jnp.broadcast_to(ai, (1, seq) + ai.shape)

    def combine(e1, e2):
        a1r, a1i, b1r, b1i = e1
        a2r, a2i, b2r, b2i = e2
        return (a2r * a1r - a2i * a1i, a2r * a1i + a2i * a1r,
                a2r * b1r - a2i * b1i + b2r, a2r * b1i + a2i * b1r + b2i)

    _, _, sr, si = lax.associative_scan(combine, (a_r, a_i, xr, xi), axis=1)
    y = (jnp.einsum('gcn,bsgn->bsgc', c_re.astype(f32), sr)
         - jnp.einsum('gcn,bsgn->bsgc', c_im.astype(f32), si))
    y = y.reshape(bsz, seq, SSM_WIDTH) + d_skip.astype(f32) * uf
    g = jax.nn.gelu(y)
    out = g * jax.nn.sigmoid(g @ w_glu.astype(f32) + b_glu.astype(f32))
    return out.astype(u.dtype)


def mlstm_mixer(q, k, v, i_pre, f_pre):
    f32 = jnp.float32
    bsz, seq, nh, _ = q.shape
    L = MLSTM_CHUNK
    nc = seq // L
    q = q.astype(f32) * (MLSTM_QK ** -0.5)
    log_f = jax.nn.log_sigmoid(f_pre.astype(f32))
    log_i = i_pre.astype(f32)

    def to_chunks(t):
        return t.reshape((bsz, nc, L) + t.shape[2:]).swapaxes(0, 1)

    causal = jnp.tril(jnp.ones((L, L), dtype=bool))

    def step(carry, inp):
        C, n, m = carry
        qc, kc, vc, lic, lfc = inp
        bh = jnp.cumsum(lfc, axis=1).transpose(0, 2, 1)
        lih = lic.transpose(0, 2, 1)
        dmat = jnp.where(causal, bh[..., :, None] - bh[..., None, :] + lih[..., None, :], -jnp.inf)
        inter = bh + m[..., None]
        m_row = jnp.maximum(jnp.max(dmat, axis=-1), inter)
        s = jnp.einsum('blhd,bshd->bhls', qc, kc) * jnp.exp(dmat - m_row[..., None])
        sc = jnp.exp(inter - m_row)
        num = (jnp.einsum('bhls,bshv->blhv', s, vc)
               + sc.transpose(0, 2, 1)[..., None] * jnp.einsum('blhd,bhdv->blhv', qc, C))
        den = jnp.sum(s, axis=-1) + sc * jnp.einsum('blhd,bhd->bhl', qc, n)
        den = jnp.maximum(jnp.abs(den), jnp.exp(-m_row))
        hout = num / den.transpose(0, 2, 1)[..., None]
        b_last = bh[..., -1]
        g = b_last[..., None] - bh + lih
        m_new = jnp.maximum(b_last + m, jnp.max(g, axis=-1))
        wk = jnp.exp(g - m_new[..., None])
        dec = jnp.exp(b_last + m - m_new)
        C_new = dec[..., None, None] * C + jnp.einsum('bhs,bshd,bshv->bhdv', wk, kc, vc)
        n_new = dec[..., None] * n + jnp.einsum('bhs,bshd->bhd', wk, kc)
        return (C_new, n_new, m_new), hout

    dk, dv = q.shape[-1], v.shape[-1]
    init = (jnp.zeros((bsz, nh, dk, dv), f32), jnp.zeros((bsz, nh, dk), f32), jnp.zeros((bsz, nh), f32))
    _, hs = lax.scan(step, init, (to_chunks(q), to_chunks(k.astype(f32)), to_chunks(v.astype(f32)),
                                  to_chunks(log_i), to_chunks(log_f)))
    return hs.swapaxes(0, 1).reshape(bsz, seq, nh, dv)


def mla_mixer(c_q, c_kv, k_rope, q_norm, kv_norm, w_uq, w_ukv, cos, sin):
    f32 = jnp.float32
    bsz, seq, _ = c_q.shape
    q = (rms_norm(c_q, q_norm) @ w_uq).astype(f32).reshape(bsz, seq, MLA_HEADS, MLA_NOPE + MLA_ROPE)
    q_nope = q[..., :MLA_NOPE]
    q_rope = apply_rope(q[..., MLA_NOPE:], cos, sin)
    kv = (rms_norm(c_kv, kv_norm) @ w_ukv).astype(f32).reshape(bsz, seq, MLA_HEADS, MLA_NOPE + MLA_V)
    k_nope = kv[..., :MLA_NOPE]
    v = kv[..., MLA_NOPE:]
    kr = apply_rope(k_rope[:, :, None, :], cos, sin)[:, :, 0, :]
    scale = (MLA_NOPE + MLA_ROPE) ** -0.5
    nb = seq // MLA_QBLOCK
    qn_b = q_nope.reshape(bsz, nb, MLA_QBLOCK, MLA_HEADS, MLA_NOPE).swapaxes(0, 1)
    qr_b = q_rope.reshape(bsz, nb, MLA_QBLOCK, MLA_HEADS, MLA_ROPE).swapaxes(0, 1)
    kpos = jnp.arange(seq)

    def block(args):
        qn, qr, blk = args
        s = (jnp.einsum('bqhd,bkhd->bhqk', qn, k_nope)
             + jnp.einsum('bqhd,bkd->bhqk', qr, kr)) * scale
        qpos = blk * MLA_QBLOCK + jnp.arange(MLA_QBLOCK)
        s = jnp.where(kpos[None, :] <= qpos[:, None], s, -jnp.inf)
        pr = jax.nn.softmax(s, axis=-1)
        return jnp.einsum('bhqk,bkhv->bqhv', pr, v)

    o = lax.map(block, (qn_b, qr_b, jnp.arange(nb)))
    return o.swapaxes(0, 1).reshape(bsz, seq, MLA_HEADS, MLA_V)


def retention_mixer(q, k, v):
    f32 = jnp.float32
    bsz, seq, nh, dk = q.shape
    dv = v.shape[-1]
    L = RET_CHUNK
    nc = seq // L
    log_g = jnp.log1p(-jnp.exp2(-5.0 - jnp.arange(nh, dtype=f32)))
    idx = jnp.arange(L)
    diff = idx[:, None] - idx[None, :]
    causal = diff >= 0
    decay_mask = jnp.where(causal, jnp.exp(jnp.where(causal, diff, 0).astype(f32) * log_g[:, None, None]), 0.0)
    q_decay = jnp.exp((idx + 1).astype(f32)[None, :] * log_g[:, None])
    k_decay = jnp.exp((L - 1 - idx).astype(f32)[None, :] * log_g[:, None])
    chunk_decay = jnp.exp(L * log_g)
    k = k.astype(f32) * (dk ** -0.5)

    def to_chunks(t):
        return t.reshape((bsz, nc, L) + t.shape[2:]).swapaxes(0, 1)

    def step(R, inp):
        qc, kc, vc = inp
        s = jnp.einsum('blhd,bshd->bhls', qc, kc) * decay_mask
        inner = jnp.einsum('bhls,bshv->blhv', s, vc)
        cross = jnp.einsum('blhd,bhdv->blhv', qc, R) * q_decay.T[None, :, :, None]
        R_new = chunk_decay[None, :, None, None] * R + jnp.einsum('bshd,bshv,hs->bhdv', kc, vc, k_decay)
        return R_new, inner + cross

    _, ys = lax.scan(step, jnp.zeros((bsz, nh, dk, dv), f32),
                     (to_chunks(q.astype(f32)), to_chunks(k), to_chunks(v.astype(f32))))
    return ys.swapaxes(0, 1).reshape(bsz, seq, nh, dv)


def hybrid_layer(h, p_l, cos, sin, ng, wg, wu, wd, w_in, w_out, mixg,
                 a_re, a_im, b_re, b_im, c_re, c_im, d_skip, log_dt, w_glu, b_glu,
                 conv_w, conv_b, b_i, b_f, q_norm, kv_norm, w_uq, w_ukv, ple_proj, ple_gate):
    bsz, seq, _ = h.shape
    h = h + 0.5 * rms_norm(swiglu(rms_norm(h, ng[0]), wg[0], wu[0], wd[0]), ng[1])

    u = rms_norm(h, ng[2])
    z = u @ w_in
    (s_u, m_q, m_k, m_v, m_o, m_i, m_f, a_cq, a_ckv, a_kr, r_q, r_k, r_v, r_g) = jnp.split(z, IN_OFFSETS, axis=-1)
    g_ssm, g_ml, g_mla, g_ret = jnp.split(mixg, MIX_OFFSETS)

    out_a = rms_norm(s5_mixer(s_u, a_re, a_im, b_re, b_im, c_re, c_im, d_skip, log_dt, w_glu, b_glu), g_ssm)

    qk = jax.nn.silu(causal_depthwise_conv(jnp.concatenate([m_q, m_k], axis=-1), conv_w, conv_b))
    m_q, m_k = jnp.split(qk, 2, axis=-1)
    hb = mlstm_mixer(m_q.reshape(bsz, seq, MLSTM_HEADS, MLSTM_QK), m_k.reshape(bsz, seq, MLSTM_HEADS, MLSTM_QK),
                     m_v.reshape(bsz, seq, MLSTM_HEADS, MLSTM_V), m_i + b_i, m_f + b_f).astype(h.dtype)
    out_b = jax.nn.sigmoid(m_o) * rms_norm(hb, g_ml.reshape(MLSTM_HEADS, MLSTM_V)).reshape(bsz, seq, -1)

    hc = mla_mixer(a_cq, a_ckv, a_kr, q_norm, kv_norm, w_uq, w_ukv, cos, sin).astype(h.dtype)
    out_c = rms_norm(hc, g_mla.reshape(MLA_HEADS, MLA_V)).reshape(bsz, seq, -1)

    rq = apply_rope(r_q.reshape(bsz, seq, RET_HEADS, RET_QK), cos, sin)
    rk = apply_rope(r_k.reshape(bsz, seq, RET_HEADS, RET_QK), cos, sin)
    hd = retention_mixer(rq, rk, r_v.reshape(bsz, seq, RET_HEADS, RET_V)).astype(h.dtype)
    out_d = jax.nn.silu(r_g) * rms_norm(hd, g_ret.reshape(RET_HEADS, RET_V)).reshape(bsz, seq, -1)

    mix = jnp.concatenate([out_a, out_b.astype(h.dtype), out_c, out_d.astype(h.dtype)], axis=-1)
    h = h + rms_norm(mix @ w_out, ng[3])

    h = h + 0.5 * rms_norm(swiglu(rms_norm(h, ng[4]), wg[1], wu[1], wd[1]), ng[5])

    gate = jax.nn.sigmoid(rms_norm(h, ng[6]) @ ple_gate)
    h = h + gate * rms_norm(p_l @ ple_proj, ng[7])
    return h


def setup_inputs(seed: int = 0) -> dict:
    key = jax.random.key(seed)
    ks = iter(jax.random.split(key, 40))
    f32 = jnp.float32

    def nrm(shape, scale):
        return jax.random.normal(next(ks), shape, f32) * scale

    x = jax.random.normal(next(ks), (BATCH, SEQ, D_MODEL), f32)
    p = jax.random.normal(next(ks), (DEPTH, BATCH, SEQ, PLE_DIM), f32)
    offs = jax.random.randint(next(ks), (BATCH, 1), 0, 4096, dtype=jnp.int32)
    positions = offs + jnp.arange(SEQ, dtype=jnp.int32)[None, :]
    norm_gains = 1.0 + nrm((DEPTH, 8, D_MODEL), 0.02)
    ffn_w_gate = nrm((DEPTH, 2, D_MODEL, D_FF), D_MODEL ** -0.5)
    ffn_w_up = nrm((DEPTH, 2, D_MODEL, D_FF), D_MODEL ** -0.5)
    ffn_w_down = nrm((DEPTH, 2, D_FF, D_MODEL), D_FF ** -0.5)
    w_in = nrm((DEPTH, D_MODEL, D_IN), D_MODEL ** -0.5)
    w_out = nrm((DEPTH, D_MIX, D_MODEL), D_MIX ** -0.5)
    mix_norm_gain = 1.0 + nrm((DEPTH, D_MIX), 0.02)
    n_idx = jnp.arange(SSM_STATE, dtype=f32)
    ssm_a_re = -0.5 * jnp.exp(nrm((DEPTH, SSM_GROUPS, SSM_STATE), 0.01))
    ssm_a_im = math.pi * n_idx + nrm((DEPTH, SSM_GROUPS, SSM_STATE), 0.01)
    ssm_b_re = nrm((DEPTH, SSM_GROUPS, SSM_STATE, SSM_GROUP), (2 * SSM_GROUP) ** -0.5)
    ssm_b_im = nrm((DEPTH, SSM_GROUPS, SSM_STATE, SSM_GROUP), (2 * SSM_GROUP) ** -0.5)
    ssm_c_re = nrm((DEPTH, SSM_GROUPS, SSM_GROUP, SSM_STATE), (2 * SSM_STATE) ** -0.5)
    ssm_c_im = nrm((DEPTH, SSM_GROUPS, SSM_GROUP, SSM_STATE), (2 * SSM_STATE) ** -0.5)
    ssm_d = nrm((DEPTH, SSM_WIDTH), 1.0)
    ssm_log_dt = jax.random.uniform(next(ks), (DEPTH, SSM_GROUPS), f32, math.log(1e-3), math.log(1e-1))
    ssm_w_glu = nrm((DEPTH, SSM_WIDTH, SSM_WIDTH), SSM_WIDTH ** -0.5)
    ssm_b_glu = nrm((DEPTH, SSM_WIDTH), 0.02)
    mlstm_conv_w = nrm((DEPTH, MLSTM_CONV, 2 * MLSTM_HEADS * MLSTM_QK), MLSTM_CONV ** -0.5)
    mlstm_conv_b = nrm((DEPTH, 2 * MLSTM_HEADS * MLSTM_QK), 0.02)
    mlstm_b_i = nrm((DEPTH, MLSTM_HEADS), 0.02)
    mlstm_b_f = jnp.linspace(3.0, 6.0, MLSTM_HEADS, dtype=f32)[None, :] + nrm((DEPTH, MLSTM_HEADS), 0.02)
    mla_q_norm = 1.0 + nrm((DEPTH, MLA_Q_RANK), 0.02)
    mla_kv_norm = 1.0 + nrm((DEPTH, MLA_KV_RANK), 0.02)
    mla_w_uq = nrm((DEPTH, MLA_Q_RANK, MLA_HEADS * (MLA_NOPE + MLA_ROPE)), MLA_Q_RANK ** -0.5)
    mla_w_ukv = nrm((DEPTH, MLA_KV_RANK, MLA_HEADS * (MLA_NOPE + MLA_V)), MLA_KV_RANK ** -0.5)
    ple_w_proj = nrm((DEPTH, PLE_DIM, D_MODEL), PLE_DIM ** -0.5)
    ple_w_gate = nrm((DEPTH, D_MODEL, D_MODEL), D_MODEL ** -0.5)
    return {"x": x, "p": p, "positions": positions, "norm_gains": norm_gains,
            "ffn_w_gate": ffn_w_gate, "ffn_w_up": ffn_w_up, "ffn_w_down": ffn_w_down,
            "w_in": w_in, "w_out": w_out, "mix_norm_gain": mix_norm_gain,
            "ssm_a_re": ssm_a_re, "ssm_a_im": ssm_a_im, "ssm_b_re": ssm_b_re, "ssm_b_im": ssm_b_im,
            "ssm_c_re": ssm_c_re, "ssm_c_im": ssm_c_im, "ssm_d": ssm_d, "ssm_log_dt": ssm_log_dt,
            "ssm_w_glu": ssm_w_glu, "ssm_b_glu": ssm_b_glu,
            "mlstm_conv_w": mlstm_conv_w, "mlstm_conv_b": mlstm_conv_b,
            "mlstm_b_i": mlstm_b_i, "mlstm_b_f": mlstm_b_f,
            "mla_q_norm": mla_q_norm, "mla_kv_norm": mla_kv_norm,
            "mla_w_uq": mla_w_uq, "mla_w_ukv": mla_w_ukv,
            "ple_w_proj": ple_w_proj, "ple_w_gate": ple_w_gate}


def reference(x, p, positions, norm_gains, ffn_w_gate, ffn_w_up, ffn_w_down, w_in, w_out, mix_norm_gain,
              ssm_a_re, ssm_a_im, ssm_b_re, ssm_b_im, ssm_c_re, ssm_c_im, ssm_d, ssm_log_dt,
              ssm_w_glu, ssm_b_glu, mlstm_conv_w, mlstm_conv_b, mlstm_b_i, mlstm_b_f,
              mla_q_norm, mla_kv_norm, mla_w_uq, mla_w_ukv, ple_w_proj, ple_w_gate):
    cos, sin = rope_angles(positions, MLA_ROPE)
    h = x
    for i in range(DEPTH):
        h = hybrid_layer(h, p[i], cos, sin, norm_gains[i], ffn_w_gate[i], ffn_w_up[i], ffn_w_down[i],
                         w_in[i], w_out[i], mix_norm_gain[i],
                         ssm_a_re[i], ssm_a_im[i], ssm_b_re[i], ssm_b_im[i], ssm_c_re[i], ssm_c_im[i],
                         ssm_d[i], ssm_log_dt[i], ssm_w_glu[i], ssm_b_glu[i],
                         mlstm_conv_w[i], mlstm_conv_b[i], mlstm_b_i[i], mlstm_b_f[i],
                         mla_q_norm[i], mla_kv_norm[i], mla_w_uq[i], mla_w_ukv[i],
                         ple_w_proj[i], ple_w_gate[i])
    return h
```

```python
import functools
import math

import jax
import jax.numpy as jnp
import numpy as np
from jax import lax
from jax.experimental import pallas as pl
from jax.experimental.pallas import tpu as pltpu

F32 = jnp.float32
BF16 = jnp.bfloat16

EPS = 1e-6
ROPE_BASE = 10000.0

D_MODEL = 2048
PLE_DIM = 256
D_FF = 5632

SSM_WIDTH = 512
SSM_GROUP = 16
SSM_GROUPS = SSM_WIDTH // SSM_GROUP
SSM_STATE = 64
SSM_LANES = SSM_GROUPS * SSM_STATE

HEADS = 4
QK = 64
HV = 128
CHUNK = 128
MLSTM_CONV = 4
MLA_Q_RANK = 384
MLA_KV_RANK = 256
MLA_NOPE = 128
MLA_QK = MLA_NOPE + QK

V7X_LANES = 128
V7X_SUBLANES = 8
V7X_VMEM_BYTES = 64 * 1024 * 1024
MIB = 1024 * 1024

Z_SU, Z_MV, Z_MO, Z_RV, Z_RG = 0, 512, 1024, 1536, 2048
Z_MQ, Z_MK, Z_CKV, Z_RQ, Z_RK = 2560, 2816, 3072, 3328, 3584
Z_CQ = 3840
Z_WIDTH = 4224
SLAB_WIDTH = 128
SLAB_I, SLAB_F = 64, 68


def _params(sem, vmem_mib):
    return pltpu.CompilerParams(dimension_semantics=sem, vmem_limit_bytes=int(vmem_mib * MIB))


def _rms(x, gain):
    ms = jnp.mean(x * x, axis=-1, keepdims=True)
    return x * lax.rsqrt(ms + EPS) * gain


def _dot(a, b):
    return jnp.dot(a, b, preferred_element_type=F32)


def _dot_nt(a, b):
    return lax.dot_general(a, b, (((1,), (1,)), ((), ())), preferred_element_type=F32)


def _dot_tn(a, b):
    return lax.dot_general(a, b, (((0,), (0,)), ((), ())), preferred_element_type=F32)


def _split_dot(tri, x, nt):
    hi = x.astype(BF16)
    lo = (x - hi.astype(F32)).astype(BF16)
    if nt:
        return _dot(hi, tri) + _dot(lo, tri)
    return _dot(tri, hi) + _dot(tri, lo)


def _rope_kernel(pos_ref, inv_ref, o_ref):
    ang = pos_ref[...].astype(F32) * inv_ref[...]
    lane = lax.broadcasted_iota(jnp.int32, ang.shape, 1)
    sgn = jnp.where(lane < 96, -1.0, 1.0).astype(F32)
    o_ref[...] = jnp.where(lane < 64, jnp.cos(ang), sgn * jnp.sin(ang))


def _rope_table(positions, tm):
    n = positions.size
    inv = 1.0 / (ROPE_BASE ** (jnp.arange(0, QK, 2, dtype=F32) / QK))
    inv = jnp.tile(inv, 4).reshape(1, 128)
    return pl.pallas_call(
        _rope_kernel,
        grid=(n // tm,),
        in_specs=[pl.BlockSpec((tm, 1), lambda i: (i, 0)),
                  pl.BlockSpec((1, 128), lambda i: (0, 0))],
        out_specs=pl.BlockSpec((tm, 128), lambda i: (i, 0)),
        out_shape=jax.ShapeDtypeStruct((n, 128), F32),
        compiler_params=_params(("parallel",), 32),
        name="rope_table",
    )(positions.reshape(n, 1), inv)


def _apply_rope(x, rope, nheads):
    width = nheads * QK
    cos = rope[:, :QK]
    sin = rope[:, QK:]
    if nheads > 1:
        cos = jnp.concatenate([cos] * nheads, axis=-1)
        sin = jnp.concatenate([sin] * nheads, axis=-1)
    if width >= V7X_LANES:
        lane = lax.broadcasted_iota(jnp.int32, x.shape, 1)
        first = (lane % QK) < (QK // 2)
        rot = jnp.where(first, pltpu.roll(x, width - QK // 2, 1), pltpu.roll(x, QK // 2, 1))
    else:
        rot = jnp.concatenate([x[:, QK // 2:], x[:, :QK // 2]], axis=-1)
    return x * cos + rot * sin


def _ffn_kernel(h_ref, gpre_ref, gpost_ref, wg_ref, wu_ref, wd_ref, o_ref, xn_ref):
    j = pl.program_id(1)

    @pl.when(j == 0)
    def _():
        xn_ref[...] = _rms(h_ref[...], gpre_ref[...]).astype(BF16)

    x = xn_ref[...]
    g = _dot(x, wg_ref[...])
    u = _dot(x, wu_ref[...])
    a = (g * jax.nn.sigmoid(g) * u).astype(BF16)
    part = _dot(a, wd_ref[...])

    @pl.when(j == 0)
    def _():
        o_ref[...] = part

    @pl.when(j > 0)
    def _():
        o_ref[...] += part

    @pl.when(j == pl.num_programs(1) - 1)
    def _():
        o_ref[...] = h_ref[...] + 0.5 * _rms(o_ref[...], gpost_ref[...])


def _ffn(h, gpre, gpost, wg, wu, wd, tm, tf):
    n, d = h.shape
    f = wg.shape[1]
    vmem = (3 * tm * d * 4 + tm * d * 2 + 6 * d * tf * 2 + 4 * tm * tf * 4) / MIB + 6
    return pl.pallas_call(
        _ffn_kernel,
        grid=(n // tm, f // tf),
        in_specs=[pl.BlockSpec((tm, d), lambda i, j: (i, 0), pipeline_mode=pl.Buffered(1)),
                  pl.BlockSpec((1, d), lambda i, j: (0, 0)),
                  pl.BlockSpec((1, d), lambda i, j: (0, 0)),
                  pl.BlockSpec((d, tf), lambda i, j: (0, j)),
                  pl.BlockSpec((d, tf), lambda i, j: (0, j)),
                  pl.BlockSpec((tf, d), lambda i, j: (j, 0))],
        out_specs=pl.BlockSpec((tm, d), lambda i, j: (i, 0)),
        out_shape=jax.ShapeDtypeStruct((n, d), F32),
        scratch_shapes=[pltpu.VMEM((tm, d), BF16)],
        compiler_params=_params(("parallel", "arbitrary"), vmem),
        name="ffn",
    )(h, gpre, gpost, wg, wu, wd)


def _inproj_kernel(h_ref, g_ref, w_ref, wgt_ref, z_ref, slab_ref, gt_ref):
    xn = _rms(h_ref[...], g_ref[...]).astype(BF16)
    z = _dot(xn, w_ref[...])
    z_ref[...] = z[:, :Z_WIDTH].astype(BF16)
    slab_ref[...] = z[:, Z_WIDTH:]
    gt_ref[...] = _dot_nt(wgt_ref[...], xn)


def _inproj(h, gain, w, wgt, tm):
    n, d = h.shape
    wtot = Z_WIDTH + SLAB_WIDTH
    vmem = (2 * tm * d * 4 + d * wtot * 2 + 2 * tm * wtot * 4 + tm * wtot * 4) / MIB + 6
    return pl.pallas_call(
        _inproj_kernel,
        grid=(n // tm,),
        in_specs=[pl.BlockSpec((tm, d), lambda i: (i, 0)),
                  pl.BlockSpec((1, d), lambda i: (0, 0)),
                  pl.BlockSpec((d, wtot), lambda i: (0, 0), pipeline_mode=pl.Buffered(1)),
                  pl.BlockSpec((8, d), lambda i: (0, 0))],
        out_specs=[pl.BlockSpec((tm, Z_WIDTH), lambda i: (i, 0)),
                   pl.BlockSpec((tm, SLAB_WIDTH), lambda i: (i, 0)),
                   pl.BlockSpec((8, tm), lambda i: (0, i))],
        out_shape=[jax.ShapeDtypeStruct((n, Z_WIDTH), BF16),
                   jax.ShapeDtypeStruct((n, SLAB_WIDTH), F32),
                   jax.ShapeDtypeStruct((8, n), F32)],
        compiler_params=_params(("parallel",), vmem),
        name="inproj",
    )(h, gain, w, wgt)


SCAN_COLS = 512


def _s5_kernel(u_ref, wb_ref, tab_ref, wc_ref, d_ref, wglu_ref, bglu_ref, gain_ref,
               o_ref, x_ref, carry_ref):
    t = u_ref.shape[0]
    nl = SSM_LANES

    @pl.when(pl.program_id(1) == 0)
    def _():
        carry_ref[...] = jnp.zeros_like(carry_ref)

    u = u_ref[...]
    x_ref[...] = _dot(u, wb_ref[...])

    for c0 in range(0, nl, SCAN_COLS):
        re = slice(c0, c0 + SCAN_COLS)
        im = slice(nl + c0, nl + c0 + SCAN_COLS)
        tabs = [(tab_ref[2 * k, :, re], tab_ref[2 * k + 1, :, re]) for k in range(4)]

        def block(i, carry, re=re, im=im, tabs=tabs):
            cr, ci = carry
            r0 = pl.multiple_of(i * V7X_SUBLANES, V7X_SUBLANES)
            xr = x_ref[pl.ds(r0, V7X_SUBLANES), re]
            xi = x_ref[pl.ds(r0, V7X_SUBLANES), im]
            for k, shift in enumerate((1, 2, 4)):
                ar, ai = tabs[k]
                rr = pltpu.roll(xr, shift, 0)
                ri = pltpu.roll(xi, shift, 0)
                xr, xi = xr + ar * rr - ai * ri, xi + ar * ri + ai * rr
            pr, pi = tabs[3]
            xr, xi = xr + pr * cr - pi * ci, xi + pr * ci + pi * cr
            x_ref[pl.ds(r0, V7X_SUBLANES), re] = xr
            x_ref[pl.ds(r0, V7X_SUBLANES), im] = xi
            last = V7X_SUBLANES - 1
            return (jnp.broadcast_to(xr[last:, :], xr.shape), jnp.broadcast_to(xi[last:, :], xi.shape))

        cr0 = carry_ref[:, re]
        ci0 = carry_ref[:, im]
        cr1, ci1 = lax.fori_loop(0, t // V7X_SUBLANES, block, (cr0, ci0))
        carry_ref[:, re] = cr1
        carry_ref[:, im] = ci1

    y = _dot(x_ref[...].astype(BF16), wc_ref[...]) + d_ref[...] * u.astype(F32)
    g = jax.nn.gelu(y)
    out = g * jax.nn.sigmoid(_dot(g.astype(BF16), wglu_ref[...]) + bglu_ref[...])
    o_ref[...] = _rms(out, gain_ref[...]).astype(o_ref.dtype)


def _s5(z, wb, tabs, wc, d_skip, wglu, bglu, gain, bsz, seq, t):
    n = bsz * seq
    nt = seq // t
    nl2 = 2 * SSM_LANES
    const = lambda b, i: (0, 0)
    vmem = (t * nl2 * 4 * 3 + 2 * SSM_WIDTH * nl2 * 2 * 2 + 8 * 8 * SSM_LANES * 4 * 2) / MIB + 8
    return pl.pallas_call(
        _s5_kernel,
        grid=(bsz, nt),
        in_specs=[pl.BlockSpec((t, SSM_WIDTH), lambda b, i: (b * nt + i, Z_SU // SSM_WIDTH)),
                  pl.BlockSpec((SSM_WIDTH, nl2), const),
                  pl.BlockSpec((8, V7X_SUBLANES, SSM_LANES), lambda b, i: (0, 0, 0)),
                  pl.BlockSpec((nl2, SSM_WIDTH), const),
                  pl.BlockSpec((1, SSM_WIDTH), const),
                  pl.BlockSpec((SSM_WIDTH, SSM_WIDTH), const),
                  pl.BlockSpec((1, SSM_WIDTH), const),
                  pl.BlockSpec((1, SSM_WIDTH), const)],
        out_specs=pl.BlockSpec((t, SSM_WIDTH), lambda b, i: (b * nt + i, 0)),
        out_shape=jax.ShapeDtypeStruct((n, SSM_WIDTH), BF16),
        scratch_shapes=[pltpu.VMEM((t, nl2), F32), pltpu.VMEM((V7X_SUBLANES, nl2), F32)],
        compiler_params=_params(("arbitrary", "arbitrary"), vmem),
        name="s5",
    )(z, wb, tabs, wc, d_skip, wglu, bglu, gain)


def _s5_tables(a_re, a_im, b_re, b_im, c_re, c_im, log_dt):
    lr, li = a_re.astype(F32), a_im.astype(F32)
    dt = jnp.exp(log_dt.astype(F32))[:, None]
    mag = jnp.exp(lr * dt)
    ar, ai = mag * jnp.cos(li * dt), mag * jnp.sin(li * dt)
    den = lr * lr + li * li
    nr, ni = ar - 1.0, ai
    zr = (nr * lr + ni * li) / den
    zi = (ni * lr - nr * li) / den
    br_, bi_ = b_re.astype(F32), b_im.astype(F32)
    bbr = zr[..., None] * br_ - zi[..., None] * bi_
    bbi = zr[..., None] * bi_ + zi[..., None] * br_
    eye = jnp.eye(SSM_GROUPS, dtype=F32)
    wb_re = jnp.einsum('gnc,gh->gchn', bbr, eye).reshape(SSM_WIDTH, SSM_LANES)
    wb_im = jnp.einsum('gnc,gh->gchn', bbi, eye).reshape(SSM_WIDTH, SSM_LANES)
    wb = jnp.concatenate([wb_re, wb_im], axis=1).astype(BF16)
    wc_re = jnp.einsum('gcn,gh->gnhc', c_re.astype(F32), eye).reshape(SSM_LANES, SSM_WIDTH)
    wc_im = jnp.einsum('gcn,gh->gnhc', c_im.astype(F32), eye).reshape(SSM_LANES, SSM_WIDTH)
    wc = jnp.concatenate([wc_re, -wc_im], axis=0).astype(BF16)

    a1 = (ar.reshape(1, SSM_LANES), ai.reshape(1, SSM_LANES))

    def cmul(p, q):
        return (p[0] * q[0] - p[1] * q[1], p[0] * q[1] + p[1] * q[0])

    pows = [a1]
    for _ in range(7):
        pows.append(cmul(pows[-1], a1))
    row = jnp.arange(V7X_SUBLANES)[:, None]
    tabs = []
    for shift in (1, 2, 4):
        pr, pi = pows[shift - 1]
        keep = row >= shift
        tabs += [jnp.where(keep, pr, 0.0), jnp.where(keep, pi, 0.0)]
    tabs += [jnp.concatenate([p[0] for p in pows], axis=0), jnp.concatenate([p[1] for p in pows], axis=0)]
    return wb, jnp.stack(tabs).astype(F32), wc


def _ret_kernel(q_ref, k_ref, v_ref, g_ref, rope_ref, dmask_ref, qdec_ref, kdec_ref, cdec_ref, gain_ref,
                o_ref, state_ref):
    t = q_ref.shape[0]

    @pl.when(pl.program_id(1) == 0)
    def _():
        state_ref[...] = jnp.zeros_like(state_ref)

    for c in range(t // CHUNK):
        rows = slice(c * CHUNK, (c + 1) * CHUNK)
        rope = rope_ref[rows, :]
        q = _apply_rope(q_ref[rows, :].astype(F32), rope, HEADS)
        k = _apply_rope(k_ref[rows, :].astype(F32), rope, HEADS) * (QK ** -0.5)
        for h in range(HEADS):
            qh = q[:, h * QK:(h + 1) * QK].astype(BF16)
            kh = k[:, h * QK:(h + 1) * QK]
            vh = v_ref[rows, h * HV:(h + 1) * HV]
            s = _dot_nt(qh, kh.astype(BF16)) * dmask_ref[h]
            inner = _dot(s.astype(BF16), vh)
            r_old = state_ref[h]
            cross = _dot(qh, r_old.astype(BF16)) * qdec_ref[h]
            state_ref[h] = cdec_ref[h] * r_old + _dot_tn((kh * kdec_ref[h]).astype(BF16), vh)
            y = _rms(inner + cross, gain_ref[:, h * HV:(h + 1) * HV])
            gate = g_ref[rows, h * HV:(h + 1) * HV].astype(F32)
            o_ref[rows, h * HV:(h + 1) * HV] = (gate * jax.nn.sigmoid(gate) * y).astype(o_ref.dtype)


def _ret_tables():
    log_g = jnp.log1p(-jnp.exp2(-5.0 - jnp.arange(HEADS, dtype=F32)))
    idx = jnp.arange(CHUNK)
    diff = idx[:, None] - idx[None, :]
    causal = diff >= 0
    dmask = jnp.where(causal, jnp.exp(jnp.where(causal, diff, 0).astype(F32) * log_g[:, None, None]), 0.0)
    q_decay = jnp.exp((idx + 1).astype(F32)[None, :] * log_g[:, None])
    k_decay = jnp.exp((CHUNK - 1 - idx).astype(F32)[None, :] * log_g[:, None])
    chunk_decay = jnp.exp(CHUNK * log_g)
    qdec = jnp.broadcast_to(q_decay[:, :, None], (HEADS, CHUNK, HV))
    kdec = jnp.broadcast_to(k_decay[:, :, None], (HEADS, CHUNK, QK))
    cdec = jnp.broadcast_to(chunk_decay[:, None, None], (HEADS, QK, HV))
    return dmask, qdec, kdec, cdec


def _retention(z, rope, gain, bsz, seq, t):
    n = bsz * seq
    nt = seq // t
    dmask, qdec, kdec, cdec = _ret_tables()
    c3 = lambda b, i: (0, 0, 0)

    def col(width, lane0):
        return pl.BlockSpec((t, width), lambda b, i: (b * nt + i, lane0 // width))

    return pl.pallas_call(
        _ret_kernel,
        grid=(bsz, nt),
        in_specs=[col(256, Z_RQ), col(256, Z_RK), col(512, Z_RV), col(512, Z_RG),
                  pl.BlockSpec((t, 128), lambda b, i: (b * nt + i, 0)),
                  pl.BlockSpec((HEADS, CHUNK, CHUNK), c3),
                  pl.BlockSpec((HEADS, CHUNK, HV), c3),
                  pl.BlockSpec((HEADS, CHUNK, QK), c3),
                  pl.BlockSpec((HEADS, QK, HV), c3),
                  pl.BlockSpec((1, HEADS * HV), lambda b, i: (0, 0))],
        out_specs=pl.BlockSpec((t, HEADS * HV), lambda b, i: (b * nt + i, 0)),
        out_shape=jax.ShapeDtypeStruct((n, HEADS * HV), BF16),
        scratch_shapes=[pltpu.VMEM((HEADS, QK, HV), F32)],
        compiler_params=_params(("arbitrary", "arbitrary"), 32),
        name="retention",
    )(z, z, z, z, rope, dmask, qdec, kdec, cdec, gain)


TAIL = V7X_SUBLANES


def _mlstm_kernel(q_ref, k_ref, v_ref, og_ref, slab_ref, gt_ref, cw_ref, cb_ref, brow_ref, bcol_ref,
                  tril_ref, triu_ref, gain_ref, o_ref, xbuf_ref, c_ref, m_ref):
    t = q_ref.shape[0]
    width = 2 * HEADS * QK

    @pl.when(pl.program_id(1) == 0)
    def _():
        xbuf_ref[0:TAIL, :] = jnp.zeros((TAIL, width), F32)
        c_ref[...] = jnp.zeros_like(c_ref)
        m_ref[...] = jnp.zeros_like(m_ref)

    xbuf_ref[TAIL:, 0:HEADS * QK] = q_ref[...].astype(F32)
    xbuf_ref[TAIL:, HEADS * QK:] = k_ref[...].astype(F32)
    acc = cb_ref[...] + cw_ref[MLSTM_CONV - 1:MLSTM_CONV, :] * xbuf_ref[TAIL:, :]
    for j in range(MLSTM_CONV - 1):
        off = TAIL - (MLSTM_CONV - 1) + j
        acc = acc + cw_ref[j:j + 1, :] * xbuf_ref[off:off + t, :]
    xbuf_ref[0:TAIL, :] = xbuf_ref[t:t + TAIL, :]
    qk = acc * jax.nn.sigmoid(acc)

    gr = gt_ref[...] + brow_ref[...]
    is_f_row = lax.broadcasted_iota(jnp.int32, gr.shape, 0) >= HEADS
    lf_r = jnp.where(is_f_row, jax.nn.log_sigmoid(gr), 0.0)
    gc = slab_ref[...] + bcol_ref[...]
    lane = lax.broadcasted_iota(jnp.int32, gc.shape, 1)
    is_f_col = (lane >= SLAB_F) & (lane < SLAB_F + HEADS)
    lf_c = jnp.where(is_f_col, jax.nn.log_sigmoid(gc), 0.0)

    row = lax.broadcasted_iota(jnp.int32, (CHUNK, CHUNK), 0)
    colid = lax.broadcasted_iota(jnp.int32, (CHUNK, CHUNK), 1)
    causal = colid <= row
    ones_col = (lax.broadcasted_iota(jnp.int32, (CHUNK, HV), 1) == 0).astype(BF16)

    for c in range(t // CHUNK):
        rows = slice(c * CHUNK, (c + 1) * CHUNK)
        bh_r = _split_dot(triu_ref[...], lf_r[:, rows], nt=True)
        bh_c = _split_dot(tril_ref[...], lf_c[rows, :], nt=False)
        a_r = gr[0:HEADS, rows] - bh_r[HEADS:, :]
        a_c = gc[rows, SLAB_I:SLAB_I + HEADS] - bh_c[:, SLAB_F:SLAB_F + HEADS]
        bh_c = bh_c[:, SLAB_F:SLAB_F + HEADS]
        for h in range(HEADS):
            qh = (qk[rows, h * QK:(h + 1) * QK] * (QK ** -0.5)).astype(BF16)
            kh = qk[rows, (HEADS + h) * QK:(HEADS + h + 1) * QK]
            vaug = jnp.concatenate([v_ref[rows, h * HV:(h + 1) * HV], ones_col], axis=-1)
            m_old = m_ref[h][0:1, 0:1]
            amat = jnp.where(causal, a_r[h:h + 1, :], -jnp.inf)
            big_m = jnp.maximum(jnp.max(amat, axis=-1, keepdims=True), m_old)
            dmat = jnp.exp(amat - big_m)
            s = _dot_nt(qh, kh.astype(BF16)) * dmat
            sc = jnp.exp(m_old - big_m)
            c_old = c_ref[h]
            tot = _dot(s.astype(BF16), vaug) + sc * _dot(qh, c_old.astype(BF16))
            num = tot[:, :HV]
            den = tot[:, HV:HV + 1]
            m_row = bh_c[:, h:h + 1] + big_m
            den = jnp.maximum(jnp.abs(den), jnp.exp(-m_row))
            hout = num / den
            m_last = big_m[CHUNK - 1:CHUNK, :]
            b_last = bh_c[CHUNK - 1:CHUNK, h:h + 1]
            wk = jnp.exp(a_c[:, h:h + 1] - m_last)
            dec = jnp.exp(m_old - m_last)
            c_ref[h] = dec * c_old + _dot_tn((kh * wk).astype(BF16), vaug)
            m_ref[h] = jnp.broadcast_to(b_last + m_last, m_ref.shape[1:])
            y = _rms(hout, gain_ref[:, h * HV:(h + 1) * HV])
            og = og_ref[rows, h * HV:(h + 1) * HV].astype(F32)
            o_ref[rows, h * HV:(h + 1) * HV] = (jax.nn.sigmoid(og) * y).astype(o_ref.dtype)


def _mlstm(z, slab, gt, conv_w, conv_b, b_i, b_f, gain, bsz, seq, t):
    n = bsz * seq
    nt = seq // t
    idx = jnp.arange(CHUNK)
    tril = (idx[None, :] <= idx[:, None]).astype(BF16)
    triu = tril.T
    brow = jnp.concatenate([b_i, b_f]).astype(F32).reshape(2 * HEADS, 1)
    bcol = jnp.zeros((1, SLAB_WIDTH), F32).at[0, SLAB_I:SLAB_I + 2 * HEADS].set(brow[:, 0])
    c2 = lambda b, i: (0, 0)

    def col(width, lane0):
        return pl.BlockSpec((t, width), lambda b, i: (b * nt + i, lane0 // width))

    return pl.pallas_call(
        _mlstm_kernel,
        grid=(bsz, nt),
        in_specs=[col(256, Z_MQ), col(256, Z_MK), col(512, Z_MV), col(512, Z_MO),
                  pl.BlockSpec((t, SLAB_WIDTH), lambda b, i: (b * nt + i, 0)),
                  pl.BlockSpec((2 * HEADS, t), lambda b, i: (0, b * nt + i)),
                  pl.BlockSpec((MLSTM_CONV, 2 * HEADS * QK), c2),
                  pl.BlockSpec((1, 2 * HEADS * QK), c2),
                  pl.BlockSpec((2 * HEADS, 1), c2),
                  pl.BlockSpec((1, SLAB_WIDTH), c2),
                  pl.BlockSpec((CHUNK, CHUNK), c2),
                  pl.BlockSpec((CHUNK, CHUNK), c2),
                  pl.BlockSpec((1, HEADS * HV), c2)],
        out_specs=pl.BlockSpec((t, HEADS * HV), lambda b, i: (b * nt + i, 0)),
        out_shape=jax.ShapeDtypeStruct((n, HEADS * HV), BF16),
        scratch_shapes=[pltpu.VMEM((t + TAIL, 2 * HEADS * QK), F32),
                        pltpu.VMEM((HEADS, QK, 2 * HV), F32),
                        pltpu.VMEM((HEADS, V7X_SUBLANES, V7X_LANES), F32)],
        compiler_params=_params(("arbitrary", "arbitrary"), 32),
        name="mlstm",
    )(z, z, z, z, slab, gt, conv_w.astype(F32), conv_b.astype(F32).reshape(1, -1), brow, bcol, tril, triu, gain)


def _mla_proj_kernel(cq_ref, ckv_ref, slab_ref, rope_ref, qn_ref, kvn_ref, wuq_ref, wukv_ref,
                     q_ref, k_ref, v_ref):
    rope = rope_ref[...]
    cq = _rms(cq_ref[...].astype(F32), qn_ref[...]).astype(BF16)
    ckv = _rms(ckv_ref[...].astype(F32), kvn_ref[...]).astype(BF16)
    q = _dot(cq, wuq_ref[...])
    kv = _dot(ckv, wukv_ref[...])
    scale = MLA_QK ** -0.5
    q_rope = _apply_rope(q[:, HEADS * MLA_NOPE:], rope, HEADS)
    k_rope = _apply_rope(slab_ref[:, 0:QK], rope, 1)
    for h in range(HEADS):
        q_ref[0, h, :, 0:MLA_NOPE] = (q[:, h * MLA_NOPE:(h + 1) * MLA_NOPE] * scale).astype(BF16)
        q_ref[0, h, :, MLA_NOPE:] = (q_rope[:, h * QK:(h + 1) * QK] * scale).astype(BF16)
        k_ref[0, h, :, 0:MLA_NOPE] = kv[:, h * MLA_NOPE:(h + 1) * MLA_NOPE].astype(BF16)
        k_ref[0, h, :, MLA_NOPE:] = k_rope.astype(BF16)
        v_ref[0, h] = kv[:, (HEADS + h) * HV:(HEADS + h + 1) * HV].astype(BF16)


def _mla_proj(z, slab, rope, q_norm, kv_norm, wuq, wukv, bsz, seq, t):
    nt = seq // t
    c2 = lambda b, i: (0, 0)
    return pl.pallas_call(
        _mla_proj_kernel,
        grid=(bsz, nt),
        in_specs=[pl.BlockSpec((t, MLA_Q_RANK), lambda b, i: (b * nt + i, Z_CQ // MLA_Q_RANK)),
                  pl.BlockSpec((t, MLA_KV_RANK), lambda b, i: (b * nt + i, Z_CKV // MLA_KV_RANK)),
                  pl.BlockSpec((t, SLAB_WIDTH), lambda b, i: (b * nt + i, 0)),
                  pl.BlockSpec((t, 128), lambda b, i: (b * nt + i, 0)),
                  pl.BlockSpec((1, MLA_Q_RANK), c2),
                  pl.BlockSpec((1, MLA_KV_RANK), c2),
                  pl.BlockSpec((MLA_Q_RANK, HEADS * MLA_QK), c2),
                  pl.BlockSpec((MLA_KV_RANK, HEADS * (MLA_NOPE + HV)), c2)],
        out_specs=[pl.BlockSpec((1, HEADS, t, MLA_QK), lambda b, i: (b, 0, i, 0)),
                   pl.BlockSpec((1, HEADS, t, MLA_QK), lambda b, i: (b, 0, i, 0)),
                   pl.BlockSpec((1, HEADS, t, HV), lambda b, i: (b, 0, i, 0))],
        out_shape=[jax.ShapeDtypeStruct((bsz, HEADS, seq, MLA_QK), BF16),
                   jax.ShapeDtypeStruct((bsz, HEADS, seq, MLA_QK), BF16),
                   jax.ShapeDtypeStruct((bsz, HEADS, seq, HV), BF16)],
        compiler_params=_params(("parallel", "parallel"), 40),
        name="mla_proj",
    )(z, z, slab, rope, q_norm, kv_norm, wuq, wukv)


def _flash_kernel(q_ref, k_ref, v_ref, gain_ref, o_ref, m_ref, l_ref, acc_ref):
    qi = pl.program_id(1)
    ki = pl.program_id(2)
    tq = q_ref.shape[2]
    tk = k_ref.shape[2]

    @pl.when(ki == 0)
    def _():
        m_ref[...] = jnp.full_like(m_ref, -jnp.inf)
        l_ref[...] = jnp.zeros_like(l_ref)
        acc_ref[...] = jnp.zeros_like(acc_ref)

    def step(masked):
        for h in range(HEADS):
            s = _dot_nt(q_ref[0, h], k_ref[0, h])
            if masked:
                qpos = qi * tq + lax.broadcasted_iota(jnp.int32, (tq, tk), 0)
                kpos = ki * tk + lax.broadcasted_iota(jnp.int32, (tq, tk), 1)
                s = jnp.where(kpos <= qpos, s, -jnp.inf)
            m_old = m_ref[h]
            m_new = jnp.maximum(m_old, jnp.max(s, axis=-1, keepdims=True))
            alpha = jnp.exp(m_old - m_new)
            p = jnp.exp(s - m_new)
            l_ref[h] = alpha * l_ref[h] + jnp.sum(p, axis=-1, keepdims=True)
            acc_ref[h] = alpha * acc_ref[h] + _dot(p.astype(BF16), v_ref[0, h])
            m_ref[h] = m_new

    @pl.when(ki < qi)
    def _():
        step(False)

    @pl.when(ki == qi)
    def _():
        step(True)
        for h in range(HEADS):
            y = acc_ref[h] / l_ref[h]
            o_ref[:, h * HV:(h + 1) * HV] = _rms(y, gain_ref[:, h * HV:(h + 1) * HV]).astype(o_ref.dtype)


def _flash(q, k, v, gain, t):
    bsz, _, seq, _ = q.shape
    nt = seq // t
    kv_idx = lambda b, i, j: (b, 0, jnp.minimum(i, j), 0)
    return pl.pallas_call(
        _flash_kernel,
        grid=(bsz, nt, nt),
        in_specs=[pl.BlockSpec((1, HEADS, t, MLA_QK), lambda b, i, j: (b, 0, i, 0)),
                  pl.BlockSpec((1, HEADS, t, MLA_QK), kv_idx),
                  pl.BlockSpec((1, HEADS, t, HV), kv_idx),
                  pl.BlockSpec((1, HEADS * HV), lambda b, i, j: (0, 0))],
        out_specs=pl.BlockSpec((t, HEADS * HV), lambda b, i, j: (b * nt + i, 0)),
        out_shape=jax.ShapeDtypeStruct((bsz * seq, HEADS * HV), BF16),
        scratch_shapes=[pltpu.VMEM((HEADS, t, 1), F32),
                        pltpu.VMEM((HEADS, t, 1), F32),
                        pltpu.VMEM((HEADS, t, HV), F32)],
        compiler_params=_params(("parallel", "parallel", "arbitrary"), 48),
        name="mla_flash",
    )(q, k, v, gain)


def _outproj_kernel(h_ref, a_ref, b_ref, c_ref, d_ref, w_ref, g_ref, o_ref):
    y = _dot(a_ref[...], w_ref[0:512, :])
    y += _dot(b_ref[...], w_ref[512:1024, :])
    y += _dot(c_ref[...], w_ref[1024:1536, :])
    y += _dot(d_ref[...], w_ref[1536:2048, :])
    o_ref[...] = h_ref[...] + _rms(y, g_ref[...])


def _outproj(h, mixes, w, gain, tm):
    n, d = h.shape
    row = lambda i: (i, 0)
    vmem = (4 * tm * d * 4 + d * d * 2 + 8 * tm * 512 * 2 + 2 * tm * d * 4) / MIB + 6
    return pl.pallas_call(
        _outproj_kernel,
        grid=(n // tm,),
        in_specs=[pl.BlockSpec((tm, d), row)] + [pl.BlockSpec((tm, 512), row)] * 4 +
                 [pl.BlockSpec((d, d), lambda i: (0, 0), pipeline_mode=pl.Buffered(1)),
                  pl.BlockSpec((1, d), lambda i: (0, 0))],
        out_specs=pl.BlockSpec((tm, d), row),
        out_shape=jax.ShapeDtypeStruct((n, d), F32),
        compiler_params=_params(("parallel",), vmem),
        name="outproj",
    )(h, *mixes, w, gain)


def _ple_kernel(h_ref, p_ref, g6_ref, g7_ref, wg_ref, wp_ref, o_ref):
    h = h_ref[...]
    gate = jax.nn.sigmoid(_dot(_rms(h, g6_ref[...]).astype(BF16), wg_ref[...]))
    emb = _rms(_dot(p_ref[...].astype(BF16), wp_ref[...]), g7_ref[...])
    o_ref[...] = h + gate * emb


def _ple(h, p, g6, g7, wg, wp, tm):
    n, d = h.shape
    row = lambda i: (i, 0)
    c2 = lambda i: (0, 0)
    vmem = (4 * tm * d * 4 + d * d * 2 + 2 * PLE_DIM * d * 2 + 3 * tm * d * 4) / MIB + 6
    return pl.pallas_call(
        _ple_kernel,
        grid=(n // tm,),
        in_specs=[pl.BlockSpec((tm, d), row), pl.BlockSpec((tm, PLE_DIM), row),
                  pl.BlockSpec((1, d), c2), pl.BlockSpec((1, d), c2),
                  pl.BlockSpec((d, d), c2, pipeline_mode=pl.Buffered(1)),
                  pl.BlockSpec((PLE_DIM, d), c2)],
        out_specs=pl.BlockSpec((tm, d), row),
        out_shape=jax.ShapeDtypeStruct((n, d), F32),
        compiler_params=_params(("parallel",), vmem),
        name="ple",
    )(h, p, g6, g7, wg, wp)


def _permute_w_in(w_in):
    sizes = (SSM_WIDTH, 256, 256, 512, 512, 4, 4, MLA_Q_RANK, MLA_KV_RANK, QK, 256, 256, 512, 512)
    offs = np.concatenate([[0], np.cumsum(sizes)])
    (s_u, m_q, m_k, m_v, m_o, m_i, m_f, a_cq, a_ckv, a_kr, r_q, r_k, r_v, r_g) = [
        w_in[:, int(offs[i]):int(offs[i + 1])] for i in range(len(sizes))]
    pad = jnp.zeros((w_in.shape[0], SLAB_WIDTH - QK - 2 * HEADS), w_in.dtype)
    main = jnp.concatenate([s_u, m_v, m_o, r_v, r_g, m_q, m_k, a_ckv, r_q, r_k, a_cq,
                            a_kr, m_i, m_f, pad], axis=1).astype(BF16)
    gates_t = jnp.concatenate([m_i, m_f], axis=1).T.astype(BF16)
    return main, gates_t


def _permute_mla(w_uq, w_ukv):
    uq = w_uq.reshape(MLA_Q_RANK, HEADS, MLA_QK)
    uq = jnp.concatenate([uq[:, :, :MLA_NOPE].reshape(MLA_Q_RANK, -1),
                          uq[:, :, MLA_NOPE:].reshape(MLA_Q_RANK, -1)], axis=1)
    ukv = w_ukv.reshape(MLA_KV_RANK, HEADS, MLA_NOPE + HV)
    ukv = jnp.concatenate([ukv[:, :, :MLA_NOPE].reshape(MLA_KV_RANK, -1),
                           ukv[:, :, MLA_NOPE:].reshape(MLA_KV_RANK, -1)], axis=1)
    return uq.astype(BF16), ukv.astype(BF16)


def _tile(n, want):
    t = min(n, want)
    assert n % t == 0, (n, t)
    return t


def kernel(x, p, positions, norm_gains, ffn_w_gate, ffn_w_up, ffn_w_down, w_in, w_out, mix_norm_gain, ssm_a_re, ssm_a_im, ssm_b_re, ssm_b_im, ssm_c_re, ssm_c_im, ssm_d, ssm_log_dt, ssm_w_glu, ssm_b_glu, mlstm_conv_w, mlstm_conv_b, mlstm_b_i, mlstm_b_f, mla_q_norm, mla_kv_norm, mla_w_uq, mla_w_ukv, ple_w_proj, ple_w_gate):
    bsz, seq, d = x.shape
    n = bsz * seq
    depth = p.shape[0]
    assert d == D_MODEL and seq % CHUNK == 0

    tm_ffn = _tile(n, 512)
    tf_ffn = _tile(D_FF, 512)
    tm_proj = _tile(n, 256)
    tm_row = _tile(n, 512)
    t_seq = _tile(seq, 512)
    t_s5 = _tile(seq, 256)
    t_att = _tile(seq, 1024)

    rope = _rope_table(positions, _tile(n, 2048))
    h = x.reshape(n, d).astype(F32)
    gains = norm_gains.astype(F32).reshape(depth, 8, 1, d)
    mixg = mix_norm_gain.astype(F32).reshape(depth, 4, 1, 512)
    wg_all, wu_all, wd_all = ffn_w_gate.astype(BF16), ffn_w_up.astype(BF16), ffn_w_down.astype(BF16)

    for i in range(depth):
        g = gains[i]
        h = _ffn(h, g[0], g[1], wg_all[i, 0], wu_all[i, 0], wd_all[i, 0], tm_ffn, tf_ffn)

        w_main, w_gt = _permute_w_in(w_in[i])
        z, slab, gt = _inproj(h, g[2], w_main, w_gt, tm_proj)

        wb, tabs, wc = _s5_tables(ssm_a_re[i], ssm_a_im[i], ssm_b_re[i], ssm_b_im[i],
                                  ssm_c_re[i], ssm_c_im[i], ssm_log_dt[i])
        out_a = _s5(z, wb, tabs, wc, ssm_d[i].astype(F32).reshape(1, -1), ssm_w_glu[i].astype(BF16),
                    ssm_b_glu[i].astype(F32).reshape(1, -1), mixg[i, 0], bsz, seq, t_s5)
        out_b = _mlstm(z, slab, gt, mlstm_conv_w[i], mlstm_conv_b[i], mlstm_b_i[i], mlstm_b_f[i],
                       mixg[i, 1], bsz, seq, t_seq)
        wuq, wukv = _permute_mla(mla_w_uq[i], mla_w_ukv[i])
        q, k, v = _mla_proj(z, slab, rope, mla_q_norm[i].astype(F32).reshape(1, -1),
                            mla_kv_norm[i].astype(F32).reshape(1, -1), wuq, wukv, bsz, seq, t_seq)
        out_c = _flash(q, k, v, mixg[i, 2], t_att)
        out_d = _retention(z, rope, mixg[i, 3], bsz, seq, t_seq)

        h = _outproj(h, (out_a, out_b, out_c, out_d), w_out[i].astype(BF16), g[3], tm_row)
        h = _ffn(h, g[4], g[5], wg_all[i, 1], wu_all[i, 1], wd_all[i, 1], tm_ffn, tf_ffn)
        h = _ple(h, p[i].reshape(n, PLE_DIM), g[6], g[7], ple_w_gate[i].astype(BF16),
                 ple_w_proj[i].astype(BF16), tm_row)
    return h.reshape(bsz, seq, d).astype(x.dtype)
```

```python
import functools
import math

import jax
import jax.numpy as jnp
import numpy as np
from jax import lax
from jax.experimental import pallas as pl
from jax.experimental.pallas import tpu as pltpu

F32 = jnp.float32
BF16 = jnp.bfloat16

EPS = 1e-6
ROPE_BASE = 10000.0

D_MODEL = 2048
PLE_DIM = 256
D_FF = 5632

SSM_WIDTH = 512
SSM_GROUP = 16
SSM_GROUPS = SSM_WIDTH // SSM_GROUP
SSM_STATE = 64
SSM_LANES = SSM_GROUPS * SSM_STATE

HEADS = 4
QK = 64
HV = 128
CHUNK = 128
MLSTM_CONV = 4
MLA_Q_RANK = 384
MLA_KV_RANK = 256
MLA_NOPE = 128
MLA_QK = MLA_NOPE + QK

V7X_LANES = 128
V7X_SUBLANES = 8
V7X_VMEM_BYTES = 64 * 1024 * 1024
MIB = 1024 * 1024

Z_SU, Z_MV, Z_MO, Z_RV, Z_RG = 0, 512, 1024, 1536, 2048
Z_MQ, Z_MK, Z_CKV, Z_RQ, Z_RK = 2560, 2816, 3072, 3328, 3584
Z_CQ = 3840
Z_WIDTH = 4224
SLAB_WIDTH = 128
SLAB_I, SLAB_F = 64, 68


def _params(sem, vmem_mib):
    return pltpu.CompilerParams(dimension_semantics=sem, vmem_limit_bytes=int(vmem_mib * MIB))


def _rms(x, gain):
    ms = jnp.mean(x * x, axis=-1, keepdims=True)
    return x * lax.rsqrt(ms + EPS) * gain


def _dot(a, b):
    return jnp.dot(a, b, preferred_element_type=F32)


def _dot_nt(a, b):
    return lax.dot_general(a, b, (((1,), (1,)), ((), ())), preferred_element_type=F32)


def _dot_tn(a, b):
    return lax.dot_general(a, b, (((0,), (0,)), ((), ())), preferred_element_type=F32)


def _split_dot(tri, x, nt):
    hi = x.astype(BF16)
    lo = (x - hi.astype(F32)).astype(BF16)
    if nt:
        return _dot(hi, tri) + _dot(lo, tri)
    return _dot(tri, hi) + _dot(tri, lo)


def _rope_kernel(pos_ref, inv_ref, o_ref):
    ang = pos_ref[...].astype(F32) * inv_ref[...]
    lane = lax.broadcasted_iota(jnp.int32, ang.shape, 1)
    sgn = jnp.where(lane < 96, -1.0, 1.0).astype(F32)
    o_ref[...] = jnp.where(lane < 64, jnp.cos(ang), sgn * jnp.sin(ang))


def _rope_table(positions, tm):
    n = positions.size
    inv = 1.0 / (ROPE_BASE ** (jnp.arange(0, QK, 2, dtype=F32) / QK))
    inv = jnp.tile(inv, 4).reshape(1, 128)
    return pl.pallas_call(
        _rope_kernel,
        grid=(n // tm,),
        in_specs=[pl.BlockSpec((tm, 1), lambda i: (i, 0)),
                  pl.BlockSpec((1, 128), lambda i: (0, 0))],
        out_specs=pl.BlockSpec((tm, 128), lambda i: (i, 0)),
        out_shape=jax.ShapeDtypeStruct((n, 128), F32),
        compiler_params=_params(("parallel",), 32),
        name="rope_table",
    )(positions.reshape(n, 1), inv)


def _apply_rope(x, rope, nheads):
    width = nheads * QK
    cos = rope[:, :QK]
    sin = rope[:, QK:]
    if nheads > 1:
        cos = jnp.concatenate([cos] * nheads, axis=-1)
        sin = jnp.concatenate([sin] * nheads, axis=-1)
    if width >= V7X_LANES:
        lane = lax.broadcasted_iota(jnp.int32, x.shape, 1)
        first = (lane % QK) < (QK // 2)
        rot = jnp.where(first, pltpu.roll(x, width - QK // 2, 1), pltpu.roll(x, QK // 2, 1))
    else:
        rot = jnp.concatenate([x[:, QK // 2:], x[:, :QK // 2]], axis=-1)
    return x * cos + rot * sin


def _ffn_kernel(h_ref, gpre_ref, gpost_ref, wg_ref, wu_ref, wd_ref, o_ref, xn_ref):
    j = pl.program_id(1)

    @pl.when(j == 0)
    def _():
        xn_ref[...] = _rms(h_ref[...], gpre_ref[...]).astype(BF16)
        o_ref[...] = jnp.zeros_like(o_ref)

    x = xn_ref[...]
    g = _dot(x, wg_ref[...])
    u = _dot(x, wu_ref[...])
    a = (g * jax.nn.sigmoid(g) * u).astype(BF16)
    o_ref[...] += _dot(a, wd_ref[...])

    @pl.when(j == pl.num_programs(1) - 1)
    def _():
        o_ref[...] = h_ref[...] + 0.5 * _rms(o_ref[...], gpost_ref[...])


def _ffn(h, gpre, gpost, wg, wu, wd, tm, tf):
    n, d = h.shape
    f = wg.shape[1]
    vmem = (4 * tm * d * 4 + tm * d * 2 + 6 * d * tf * 2 + 6 * tm * tf * 4) / MIB + 4
    return pl.pallas_call(
        _ffn_kernel,
        grid=(n // tm, f // tf),
        in_specs=[pl.BlockSpec((tm, d), lambda i, j: (i, 0), pipeline_mode=pl.Buffered(1)),
                  pl.BlockSpec((1, d), lambda i, j: (0, 0)),
                  pl.BlockSpec((1, d), lambda i, j: (0, 0)),
                  pl.BlockSpec((d, tf), lambda i, j: (0, j)),
                  pl.BlockSpec((d, tf), lambda i, j: (0, j)),
                  pl.BlockSpec((tf, d), lambda i, j: (j, 0))],
        out_specs=pl.BlockSpec((tm, d), lambda i, j: (i, 0)),
        out_shape=jax.ShapeDtypeStruct((n, d), F32),
        scratch_shapes=[pltpu.VMEM((tm, d), BF16)],
        compiler_params=_params(("parallel", "arbitrary"), vmem),
        name="ffn",
    )(h, gpre, gpost, wg, wu, wd)


def _inproj_kernel(h_ref, g_ref, w_ref, wgt_ref, z_ref, slab_ref, gt_ref):
    xn = _rms(h_ref[...], g_ref[...]).astype(BF16)
    z = _dot(xn, w_ref[...])
    z_ref[...] = z[:, :Z_WIDTH].astype(BF16)
    slab_ref[...] = z[:, Z_WIDTH:]
    gt_ref[...] = _dot_nt(wgt_ref[...], xn)


def _inproj(h, gain, w, wgt, tm):
    n, d = h.shape
    wtot = Z_WIDTH + SLAB_WIDTH
    vmem = (2 * tm * d * 4 + d * wtot * 2 + 2 * tm * wtot * 4 + tm * wtot * 4) / MIB + 6
    return pl.pallas_call(
        _inproj_kernel,
        grid=(n // tm,),
        in_specs=[pl.BlockSpec((tm, d), lambda i: (i, 0)),
                  pl.BlockSpec((1, d), lambda i: (0, 0)),
                  pl.BlockSpec((d, wtot), lambda i: (0, 0), pipeline_mode=pl.Buffered(1)),
                  pl.BlockSpec((8, d), lambda i: (0, 0))],
        out_specs=[pl.BlockSpec((tm, Z_WIDTH), lambda i: (i, 0)),
                   pl.BlockSpec((tm, SLAB_WIDTH), lambda i: (i, 0)),
                   pl.BlockSpec((8, tm), lambda i: (0, i))],
        out_shape=[jax.ShapeDtypeStruct((n, Z_WIDTH), BF16),
                   jax.ShapeDtypeStruct((n, SLAB_WIDTH), F32),
                   jax.ShapeDtypeStruct((8, n), F32)],
        compiler_params=_params(("parallel",), vmem),
        name="inproj",
    )(h, gain, w, wgt)


SCAN_COLS = 512


def _s5_kernel(u_ref, wb_ref, tab_ref, wc_ref, d_ref, wglu_ref, bglu_ref, gain_ref,
               o_ref, x_ref, carry_ref):
    t = u_ref.shape[0]
    nl = SSM_LANES

    @pl.when(pl.program_id(1) == 0)
    def _():
        carry_ref[...] = jnp.zeros_like(carry_ref)

    u = u_ref[...]
    x_ref[...] = _dot(u, wb_ref[...])

    for c0 in range(0, nl, SCAN_COLS):
        re = slice(c0, c0 + SCAN_COLS)
        im = slice(nl + c0, nl + c0 + SCAN_COLS)
        tabs = [(tab_ref[2 * k, :, re], tab_ref[2 * k + 1, :, re]) for k in range(4)]

        def block(i, carry, re=re, im=im, tabs=tabs):
            cr, ci = carry
            r0 = pl.multiple_of(i * V7X_SUBLANES, V7X_SUBLANES)
            xr = x_ref[pl.ds(r0, V7X_SUBLANES), re]
            xi = x_ref[pl.ds(r0, V7X_SUBLANES), im]
            for k, shift in enumerate((1, 2, 4)):
                ar, ai = tabs[k]
                rr = pltpu.roll(xr, shift, 0)
                ri = pltpu.roll(xi, shift, 0)
                xr, xi = xr + ar * rr - ai * ri, xi + ar * ri + ai * rr
            pr, pi = tabs[3]
            xr, xi = xr + pr * cr - pi * ci, xi + pr * ci + pi * cr
            x_ref[pl.ds(r0, V7X_SUBLANES), re] = xr
            x_ref[pl.ds(r0, V7X_SUBLANES), im] = xi
            last = V7X_SUBLANES - 1
            return (jnp.broadcast_to(xr[last:, :], xr.shape), jnp.broadcast_to(xi[last:, :], xi.shape))

        cr0 = carry_ref[:, re]
        ci0 = carry_ref[:, im]
        cr1, ci1 = lax.fori_loop(0, t // V7X_SUBLANES, block, (cr0, ci0))
        carry_ref[:, re] = cr1
        carry_ref[:, im] = ci1

    y = _dot(x_ref[...].astype(BF16), wc_ref[...]) + d_ref[...] * u.astype(F32)
    g = jax.nn.gelu(y)
    out = g * jax.nn.sigmoid(_dot(g.astype(BF16), wglu_ref[...]) + bglu_ref[...])
    o_ref[...] = _rms(out, gain_ref[...]).astype(o_ref.dtype)


def _s5(z, wb, tabs, wc, d_skip, wglu, bglu, gain, bsz, seq, t):
    n = bsz * seq
    nt = seq // t
    nl2 = 2 * SSM_LANES
    const = lambda b, i: (0, 0)
    vmem = (t * nl2 * 4 * 3 + 2 * SSM_WIDTH * nl2 * 2 * 2 + 8 * 8 * SSM_LANES * 4 * 2) / MIB + 8
    return pl.pallas_call(
        _s5_kernel,
        grid=(bsz, nt),
        in_specs=[pl.BlockSpec((t, SSM_WIDTH), lambda b, i: (b * nt + i, Z_SU // SSM_WIDTH)),
                  pl.BlockSpec((SSM_WIDTH, nl2), const),
                  pl.BlockSpec((8, V7X_SUBLANES, SSM_LANES), lambda b, i: (0, 0, 0)),
                  pl.BlockSpec((nl2, SSM_WIDTH), const),
                  pl.BlockSpec((1, SSM_WIDTH), const),
                  pl.BlockSpec((SSM_WIDTH, SSM_WIDTH), const),
                  pl.BlockSpec((1, SSM_WIDTH), const),
                  pl.BlockSpec((1, SSM_WIDTH), const)],
        out_specs=pl.BlockSpec((t, SSM_WIDTH), lambda b, i: (b * nt + i, 0)),
        out_shape=jax.ShapeDtypeStruct((n, SSM_WIDTH), BF16),
        scratch_shapes=[pltpu.VMEM((t, nl2), F32), pltpu.VMEM((V7X_SUBLANES, nl2), F32)],
        compiler_params=_params(("arbitrary", "arbitrary"), vmem),
        name="s5",
    )(z, wb, tabs, wc, d_skip, wglu, bglu, gain)


def _s5_tables(a_re, a_im, b_re, b_im, c_re, c_im, log_dt):
    lr, li = a_re.astype(F32), a_im.astype(F32)
    dt = jnp.exp(log_dt.astype(F32))[:, None]
    mag = jnp.exp(lr * dt)
    ar, ai = mag * jnp.cos(li * dt), mag * jnp.sin(li * dt)
    den = lr * lr + li * li
    nr, ni = ar - 1.0, ai
    zr = (nr * lr + ni * li) / den
    zi = (ni * lr - nr * li) / den
    br_, bi_ = b_re.astype(F32), b_im.astype(F32)
    bbr = zr[..., None] * br_ - zi[..., None] * bi_
    bbi = zr[..., None] * bi_ + zi[..., None] * br_
    eye = jnp.eye(SSM_GROUPS, dtype=F32)
    wb_re = jnp.einsum('gnc,gh->gchn', bbr, eye).reshape(SSM_WIDTH, SSM_LANES)
    wb_im = jnp.einsum('gnc,gh->gchn', bbi, eye).reshape(SSM_WIDTH, SSM_LANES)
    wb = jnp.concatenate([wb_re, wb_im], axis=1).astype(BF16)
    wc_re = jnp.einsum('gcn,gh->gnhc', c_re.astype(F32), eye).reshape(SSM_LANES, SSM_WIDTH)
    wc_im = jnp.einsum('gcn,gh->gnhc', c_im.astype(F32), eye).reshape(SSM_LANES, SSM_WIDTH)
    wc = jnp.concatenate([wc_re, -wc_im], axis=0).astype(BF16)

    a1 = (ar.reshape(1, SSM_LANES), ai.reshape(1, SSM_LANES))

    def cmul(p, q):
        return (p[0] * q[0] - p[1] * q[1], p[0] * q[1] + p[1] * q[0])

    pows = [a1]
    for _ in range(7):
        pows.append(cmul(pows[-1], a1))
    row = jnp.arange(V7X_SUBLANES)[:, None]
    tabs = []
    for shift in (1, 2, 4):
        pr, pi = pows[shift - 1]
        keep = row >= shift
        tabs += [jnp.where(keep, pr, 0.0), jnp.where(keep, pi, 0.0)]
    tabs += [jnp.concatenate([p[0] for p in pows], axis=0), jnp.concatenate([p[1] for p in pows], axis=0)]
    return wb, jnp.stack(tabs).astype(F32), wc


def _ret_kernel(q_ref, k_ref, v_ref, g_ref, rope_ref, dmask_ref, qdec_ref, kdec_ref, cdec_ref, gain_ref,
                o_ref, state_ref):
    t = q_ref.shape[0]

    @pl.when(pl.program_id(1) == 0)
    def _():
        state_ref[...] = jnp.zeros_like(state_ref)

    for c in range(t // CHUNK):
        rows = slice(c * CHUNK, (c + 1) * CHUNK)
        rope = rope_ref[rows, :]
        q = _apply_rope(q_ref[rows, :].astype(F32), rope, HEADS)
        k = _apply_rope(k_ref[rows, :].astype(F32), rope, HEADS) * (QK ** -0.5)
        for h in range(HEADS):
            qh = q[:, h * QK:(h + 1) * QK].astype(BF16)
            kh = k[:, h * QK:(h + 1) * QK]
            vh = v_ref[rows, h * HV:(h + 1) * HV]
            s = _dot_nt(qh, kh.astype(BF16)) * dmask_ref[h]
            inner = _dot(s.astype(BF16), vh)
            r_old = state_ref[h]
            cross = _dot(qh, r_old.astype(BF16)) * qdec_ref[h]
            state_ref[h] = cdec_ref[h] * r_old + _dot_tn((kh * kdec_ref[h]).astype(BF16), vh)
            y = _rms(inner + cross, gain_ref[:, h * HV:(h + 1) * HV])
            gate = g_ref[rows, h * HV:(h + 1) * HV].astype(F32)
            o_ref[rows, h * HV:(h + 1) * HV] = (gate * jax.nn.sigmoid(gate) * y).astype(o_ref.dtype)


def _ret_tables():
    log_g = jnp.log1p(-jnp.exp2(-5.0 - jnp.arange(HEADS, dtype=F32)))
    idx = jnp.arange(CHUNK)
    diff = idx[:, None] - idx[None, :]
    causal = diff >= 0
    dmask = jnp.where(causal, jnp.exp(jnp.where(causal, diff, 0).astype(F32) * log_g[:, None, None]), 0.0)
    q_decay = jnp.exp((idx + 1).astype(F32)[None, :] * log_g[:, None])
    k_decay = jnp.exp((CHUNK - 1 - idx).astype(F32)[None, :] * log_g[:, None])
    chunk_decay = jnp.exp(CHUNK * log_g)
    qdec = jnp.broadcast_to(q_decay[:, :, None], (HEADS, CHUNK, HV))
    kdec = jnp.broadcast_to(k_decay[:, :, None], (HEADS, CHUNK, QK))
    cdec = jnp.broadcast_to(chunk_decay[:, None, None], (HEADS, QK, HV))
    return dmask, qdec, kdec, cdec


def _retention(z, rope, gain, bsz, seq, t):
    n = bsz * seq
    nt = seq // t
    dmask, qdec, kdec, cdec = _ret_tables()
    c3 = lambda b, i: (0, 0, 0)

    def col(width, lane0):
        return pl.BlockSpec((t, width), lambda b, i: (b * nt + i, lane0 // width))

    return pl.pallas_call(
        _ret_kernel,
        grid=(bsz, nt),
        in_specs=[col(256, Z_RQ), col(256, Z_RK), col(512, Z_RV), col(512, Z_RG),
                  pl.BlockSpec((t, 128), lambda b, i: (b * nt + i, 0)),
                  pl.BlockSpec((HEADS, CHUNK, CHUNK), c3),
                  pl.BlockSpec((HEADS, CHUNK, HV), c3),
                  pl.BlockSpec((HEADS, CHUNK, QK), c3),
                  pl.BlockSpec((HEADS, QK, HV), c3),
                  pl.BlockSpec((1, HEADS * HV), lambda b, i: (0, 0))],
        out_specs=pl.BlockSpec((t, HEADS * HV), lambda b, i: (b * nt + i, 0)),
        out_shape=jax.ShapeDtypeStruct((n, HEADS * HV), BF16),
        scratch_shapes=[pltpu.VMEM((HEADS, QK, HV), F32)],
        compiler_params=_params(("arbitrary", "arbitrary"), 32),
        name="retention",
    )(z, z, z, z, rope, dmask, qdec, kdec, cdec, gain)


TAIL = V7X_SUBLANES


def _mlstm_kernel(q_ref, k_ref, v_ref, og_ref, slab_ref, gt_ref, cw_ref, cb_ref, brow_ref, bcol_ref,
                  tril_ref, triu_ref, gain_ref, o_ref, xbuf_ref, c_ref, m_ref):
    t = q_ref.shape[0]
    width = 2 * HEADS * QK

    @pl.when(pl.program_id(1) == 0)
    def _():
        xbuf_ref[0:TAIL, :] = jnp.zeros((TAIL, width), F32)
        c_ref[...] = jnp.zeros_like(c_ref)
        m_ref[...] = jnp.zeros_like(m_ref)

    xbuf_ref[TAIL:, 0:HEADS * QK] = q_ref[...].astype(F32)
    xbuf_ref[TAIL:, HEADS * QK:] = k_ref[...].astype(F32)
    acc = cb_ref[...] + cw_ref[MLSTM_CONV - 1:MLSTM_CONV, :] * xbuf_ref[TAIL:, :]
    for j in range(MLSTM_CONV - 1):
        off = TAIL - (MLSTM_CONV - 1) + j
        acc = acc + cw_ref[j:j + 1, :] * xbuf_ref[off:off + t, :]
    xbuf_ref[0:TAIL, :] = xbuf_ref[t:t + TAIL, :]
    qk = acc * jax.nn.sigmoid(acc)

    gr = gt_ref[...] + brow_ref[...]
    is_f_row = lax.broadcasted_iota(jnp.int32, gr.shape, 0) >= HEADS
    lf_r = jnp.where(is_f_row, jax.nn.log_sigmoid(gr), 0.0)
    gc = slab_ref[...] + bcol_ref[...]
    lane = lax.broadcasted_iota(jnp.int32, gc.shape, 1)
    is_f_col = (lane >= SLAB_F) & (lane < SLAB_F + HEADS)
    lf_c = jnp.where(is_f_col, jax.nn.log_sigmoid(gc), 0.0)

    row = lax.broadcasted_iota(jnp.int32, (CHUNK, CHUNK), 0)
    colid = lax.broadcasted_iota(jnp.int32, (CHUNK, CHUNK), 1)
    causal = colid <= row
    ones_col = (lax.broadcasted_iota(jnp.int32, (CHUNK, HV), 1) == 0).astype(BF16)

    for c in range(t // CHUNK):
        rows = slice(c * CHUNK, (c + 1) * CHUNK)
        bh_r = _split_dot(triu_ref[...], lf_r[:, rows], nt=True)
        bh_c = _split_dot(tril_ref[...], lf_c[rows, :], nt=False)
        a_r = gr[0:HEADS, rows] - bh_r[HEADS:, :]
        a_c = gc[rows, SLAB_I:SLAB_I + HEADS] - bh_c[:, SLAB_F:SLAB_F + HEADS]
        bh_c = bh_c[:, SLAB_F:SLAB_F + HEADS]
        for h in range(HEADS):
            qh = (qk[rows, h * QK:(h + 1) * QK] * (QK ** -0.5)).astype(BF16)
            kh = qk[rows, (HEADS + h) * QK:(HEADS + h + 1) * QK]
            vaug = jnp.concatenate([v_ref[rows, h * HV:(h + 1) * HV], ones_col], axis=-1)
            m_old = m_ref[h][0:1, 0:1]
            amat = jnp.where(causal, a_r[h:h + 1, :], -jnp.inf)
            big_m = jnp.maximum(jnp.max(amat, axis=-1, keepdims=True), m_old)
            dmat = jnp.exp(amat - big_m)
            s = _dot_nt(qh, kh.astype(BF16)) * dmat
            sc = jnp.exp(m_old - big_m)
            c_old = c_ref[h]
            tot = _dot(s.astype(BF16), vaug) + sc * _dot(qh, c_old.astype(BF16))
            num = tot[:, :HV]
            den = tot[:, HV:HV + 1]
            m_row = bh_c[:, h:h + 1] + big_m
            den = jnp.maximum(jnp.abs(den), jnp.exp(-m_row))
            hout = num / den
            m_last = big_m[CHUNK - 1:CHUNK, :]
            b_last = bh_c[CHUNK - 1:CHUNK, h:h + 1]
            wk = jnp.exp(a_c[:, h:h + 1] - m_last)
            dec = jnp.exp(m_old - m_last)
            c_ref[h] = dec * c_old + _dot_tn((kh * wk).astype(BF16), vaug)
            m_ref[h] = jnp.broadcast_to(b_last + m_last, m_ref.shape[1:])
            y = _rms(hout, gain_ref[:, h * HV:(h + 1) * HV])
            og = og_ref[rows, h * HV:(h + 1) * HV].astype(F32)
            o_ref[rows, h * HV:(h + 1) * HV] = (jax.nn.sigmoid(og) * y).astype(o_ref.dtype)


def _mlstm(z, slab, gt, conv_w, conv_b, b_i, b_f, gain, bsz, seq, t):
    n = bsz * seq
    nt = seq // t
    idx = jnp.arange(CHUNK)
    tril = (idx[None, :] <= idx[:, None]).astype(BF16)
    triu = tril.T
    brow = jnp.concatenate([b_i, b_f]).astype(F32).reshape(2 * HEADS, 1)
    bcol = jnp.zeros((1, SLAB_WIDTH), F32).at[0, SLAB_I:SLAB_I + 2 * HEADS].set(brow[:, 0])
    c2 = lambda b, i: (0, 0)

    def col(width, lane0):
        return pl.BlockSpec((t, width), lambda b, i: (b * nt + i, lane0 // width))

    return pl.pallas_call(
        _mlstm_kernel,
        grid=(bsz, nt),
        in_specs=[col(256, Z_MQ), col(256, Z_MK), col(512, Z_MV), col(512, Z_MO),
                  pl.BlockSpec((t, SLAB_WIDTH), lambda b, i: (b * nt + i, 0)),
                  pl.BlockSpec((2 * HEADS, t), lambda b, i: (0, b * nt + i)),
                  pl.BlockSpec((MLSTM_CONV, 2 * HEADS * QK), c2),
                  pl.BlockSpec((1, 2 * HEADS * QK), c2),
                  pl.BlockSpec((2 * HEADS, 1), c2),
                  pl.BlockSpec((1, SLAB_WIDTH), c2),
                  pl.BlockSpec((CHUNK, CHUNK), c2),
                  pl.BlockSpec((CHUNK, CHUNK), c2),
                  pl.BlockSpec((1, HEADS * HV), c2)],
        out_specs=pl.BlockSpec((t, HEADS * HV), lambda b, i: (b * nt + i, 0)),
        out_shape=jax.ShapeDtypeStruct((n, HEADS * HV), BF16),
        scratch_shapes=[pltpu.VMEM((t + TAIL, 2 * HEADS * QK), F32),
                        pltpu.VMEM((HEADS, QK, 2 * HV), F32),
                        pltpu.VMEM((HEADS, V7X_SUBLANES, V7X_LANES), F32)],
        compiler_params=_params(("arbitrary", "arbitrary"), 32),
        name="mlstm",
    )(z, z, z, z, slab, gt, conv_w.astype(F32), conv_b.astype(F32).reshape(1, -1), brow, bcol, tril, triu, gain)


def _mla_proj_kernel(cq_ref, ckv_ref, slab_ref, rope_ref, qn_ref, kvn_ref, wuq_ref, wukv_ref,
                     q_ref, k_ref, v_ref):
    rope = rope_ref[...]
    cq = _rms(cq_ref[...].astype(F32), qn_ref[...]).astype(BF16)
    ckv = _rms(ckv_ref[...].astype(F32), kvn_ref[...]).astype(BF16)
    q = _dot(cq, wuq_ref[...])
    kv = _dot(ckv, wukv_ref[...])
    scale = (MLA_QK ** -0.5) * math.log2(math.e)
    q_rope = _apply_rope(q[:, HEADS * MLA_NOPE:], rope, HEADS)
    k_rope = _apply_rope(slab_ref[:, 0:QK], rope, 1)
    ones_col = (lax.broadcasted_iota(jnp.int32, (q.shape[0], HV), 1) == 0).astype(BF16)
    for h in range(HEADS):
        q_ref[0, h, :, 0:MLA_NOPE] = (q[:, h * MLA_NOPE:(h + 1) * MLA_NOPE] * scale).astype(BF16)
        q_ref[0, h, :, MLA_NOPE:] = (q_rope[:, h * QK:(h + 1) * QK] * scale).astype(BF16)
        k_ref[0, h, :, 0:MLA_NOPE] = kv[:, h * MLA_NOPE:(h + 1) * MLA_NOPE].astype(BF16)
        k_ref[0, h, :, MLA_NOPE:] = k_rope.astype(BF16)
        v_ref[0, h, :, 0:HV] = kv[:, (HEADS + h) * HV:(HEADS + h + 1) * HV].astype(BF16)
        v_ref[0, h, :, HV:] = ones_col


def _mla_proj(z, slab, rope, q_norm, kv_norm, wuq, wukv, bsz, seq, t):
    nt = seq // t
    c2 = lambda b, i: (0, 0)
    return pl.pallas_call(
        _mla_proj_kernel,
        grid=(bsz, nt),
        in_specs=[pl.BlockSpec((t, MLA_Q_RANK), lambda b, i: (b * nt + i, Z_CQ // MLA_Q_RANK)),
                  pl.BlockSpec((t, MLA_KV_RANK), lambda b, i: (b * nt + i, Z_CKV // MLA_KV_RANK)),
                  pl.BlockSpec((t, SLAB_WIDTH), lambda b, i: (b * nt + i, 0)),
                  pl.BlockSpec((t, 128), lambda b, i: (b * nt + i, 0)),
                  pl.BlockSpec((1, MLA_Q_RANK), c2),
                  pl.BlockSpec((1, MLA_KV_RANK), c2),
                  pl.BlockSpec((MLA_Q_RANK, HEADS * MLA_QK), c2),
                  pl.BlockSpec((MLA_KV_RANK, HEADS * (MLA_NOPE + HV)), c2)],
        out_specs=[pl.BlockSpec((1, HEADS, t, MLA_QK), lambda b, i: (b, 0, i, 0)),
                   pl.BlockSpec((1, HEADS, t, MLA_QK), lambda b, i: (b, 0, i, 0)),
                   pl.BlockSpec((1, HEADS, t, 2 * HV), lambda b, i: (b, 0, i, 0))],
        out_shape=[jax.ShapeDtypeStruct((bsz, HEADS, seq, MLA_QK), BF16),
                   jax.ShapeDtypeStruct((bsz, HEADS, seq, MLA_QK), BF16),
                   jax.ShapeDtypeStruct((bsz, HEADS, seq, 2 * HV), BF16)],
        compiler_params=_params(("parallel", "parallel"), 40),
        name="mla_proj",
    )(z, z, slab, rope, q_norm, kv_norm, wuq, wukv)


FLASH_SUB = 512
VAUG = 2 * HV


def _flash_kernel(q_ref, k_ref, v_ref, gain_ref, o_ref, m_ref, acc_ref):
    qi = pl.program_id(1)
    ki = pl.program_id(2)
    tk = k_ref.shape[2]
    sub = min(FLASH_SUB, tk)

    @pl.when(ki == 0)
    def _():
        m_ref[...] = jnp.full_like(m_ref, -jnp.inf)
        acc_ref[...] = jnp.zeros_like(acc_ref)

    def step(diagonal):
        for h in range(HEADS):
            for c in range(tk // sub):
                r0 = c * sub if diagonal else 0
                q = q_ref[0, h, r0:, :]
                s = _dot_nt(q, k_ref[0, h, c * sub:(c + 1) * sub, :])
                if diagonal:
                    row = r0 + lax.broadcasted_iota(jnp.int32, s.shape, 0)
                    col = c * sub + lax.broadcasted_iota(jnp.int32, s.shape, 1)
                    s = jnp.where(col <= row, s, -jnp.inf)
                m_old = m_ref[h, r0:, :]
                m_new = jnp.maximum(m_old, jnp.max(s, axis=-1, keepdims=True))
                alpha = jnp.exp2(m_old - m_new)
                p = jnp.exp2(s - m_new[:, 0:1])
                pv = _dot(p.astype(BF16), v_ref[0, h, c * sub:(c + 1) * sub, :])
                acc_ref[h, r0:, :] = jnp.concatenate([alpha, alpha], axis=-1) * acc_ref[h, r0:, :] + pv
                m_ref[h, r0:, :] = m_new

    @pl.when(ki < qi)
    def _():
        step(False)

    @pl.when(ki == qi)
    def _():
        step(True)
        for h in range(HEADS):
            acc = acc_ref[h]
            y = acc[:, :HV] / acc[:, HV:HV + 1]
            o_ref[:, h * HV:(h + 1) * HV] = _rms(y, gain_ref[:, h * HV:(h + 1) * HV]).astype(o_ref.dtype)


def _flash(q, k, v, gain, t):
    bsz, _, seq, _ = q.shape
    nt = seq // t
    kv_idx = lambda b, i, j: (b, 0, jnp.minimum(i, j), 0)
    return pl.pallas_call(
        _flash_kernel,
        grid=(bsz, nt, nt),
        in_specs=[pl.BlockSpec((1, HEADS, t, MLA_QK), lambda b, i, j: (b, 0, i, 0)),
                  pl.BlockSpec((1, HEADS, t, MLA_QK), kv_idx),
                  pl.BlockSpec((1, HEADS, t, VAUG), kv_idx),
                  pl.BlockSpec((1, HEADS * HV), lambda b, i, j: (0, 0))],
        out_specs=pl.BlockSpec((t, HEADS * HV), lambda b, i, j: (b * nt + i, 0)),
        out_shape=jax.ShapeDtypeStruct((bsz * seq, HEADS * HV), BF16),
        scratch_shapes=[pltpu.VMEM((HEADS, t, V7X_LANES), F32),
                        pltpu.VMEM((HEADS, t, VAUG), F32)],
        compiler_params=_params(("parallel", "parallel", "arbitrary"), 48),
        name="mla_flash",
    )(q, k, v, gain)


def _outproj_kernel(h_ref, a_ref, b_ref, c_ref, d_ref, w_ref, g_ref, o_ref):
    y = _dot(a_ref[...], w_ref[0:512, :])
    y += _dot(b_ref[...], w_ref[512:1024, :])
    y += _dot(c_ref[...], w_ref[1024:1536, :])
    y += _dot(d_ref[...], w_ref[1536:2048, :])
    o_ref[...] = h_ref[...] + _rms(y, g_ref[...])


def _outproj(h, mixes, w, gain, tm):
    n, d = h.shape
    row = lambda i: (i, 0)
    vmem = (4 * tm * d * 4 + d * d * 2 + 8 * tm * 512 * 2 + 2 * tm * d * 4) / MIB + 6
    return pl.pallas_call(
        _outproj_kernel,
        grid=(n // tm,),
        in_specs=[pl.BlockSpec((tm, d), row)] + [pl.BlockSpec((tm, 512), row)] * 4 +
                 [pl.BlockSpec((d, d), lambda i: (0, 0), pipeline_mode=pl.Buffered(1)),
                  pl.BlockSpec((1, d), lambda i: (0, 0))],
        out_specs=pl.BlockSpec((tm, d), row),
        out_shape=jax.ShapeDtypeStruct((n, d), F32),
        compiler_params=_params(("parallel",), vmem),
        name="outproj",
    )(h, *mixes, w, gain)


def _ple_kernel(h_ref, p_ref, g6_ref, g7_ref, wg_ref, wp_ref, o_ref):
    h = h_ref[...]
    gate = jax.nn.sigmoid(_dot(_rms(h, g6_ref[...]).astype(BF16), wg_ref[...]))
    emb = _rms(_dot(p_ref[...].astype(BF16), wp_ref[...]), g7_ref[...])
    o_ref[...] = h + gate * emb


def _ple(h, p, g6, g7, wg, wp, tm):
    n, d = h.shape
    row = lambda i: (i, 0)
    c2 = lambda i: (0, 0)
    vmem = (4 * tm * d * 4 + d * d * 2 + 2 * PLE_DIM * d * 2 + 3 * tm * d * 4) / MIB + 6
    return pl.pallas_call(
        _ple_kernel,
        grid=(n // tm,),
        in_specs=[pl.BlockSpec((tm, d), row), pl.BlockSpec((tm, PLE_DIM), row),
                  pl.BlockSpec((1, d), c2), pl.BlockSpec((1, d), c2),
                  pl.BlockSpec((d, d), c2, pipeline_mode=pl.Buffered(1)),
                  pl.BlockSpec((PLE_DIM, d), c2)],
        out_specs=pl.BlockSpec((tm, d), row),
        out_shape=jax.ShapeDtypeStruct((n, d), F32),
        compiler_params=_params(("parallel",), vmem),
        name="ple",
    )(h, p, g6, g7, wg, wp)


def _permute_w_in(w_in):
    sizes = (SSM_WIDTH, 256, 256, 512, 512, 4, 4, MLA_Q_RANK, MLA_KV_RANK, QK, 256, 256, 512, 512)
    offs = np.concatenate([[0], np.cumsum(sizes)])
    (s_u, m_q, m_k, m_v, m_o, m_i, m_f, a_cq, a_ckv, a_kr, r_q, r_k, r_v, r_g) = [
        w_in[:, int(offs[i]):int(offs[i + 1])] for i in range(len(sizes))]
    pad = jnp.zeros((w_in.shape[0], SLAB_WIDTH - QK - 2 * HEADS), w_in.dtype)
    main = jnp.concatenate([s_u, m_v, m_o, r_v, r_g, m_q, m_k, a_ckv, r_q, r_k, a_cq,
                            a_kr, m_i, m_f, pad], axis=1).astype(BF16)
    gates_t = jnp.concatenate([m_i, m_f], axis=1).T.astype(BF16)
    return main, gates_t


def _permute_mla(w_uq, w_ukv):
    uq = w_uq.reshape(MLA_Q_RANK, HEADS, MLA_QK)
    uq = jnp.concatenate([uq[:, :, :MLA_NOPE].reshape(MLA_Q_RANK, -1),
                          uq[:, :, MLA_NOPE:].reshape(MLA_Q_RANK, -1)], axis=1)
    ukv = w_ukv.reshape(MLA_KV_RANK, HEADS, MLA_NOPE + HV)
    ukv = jnp.concatenate([ukv[:, :, :MLA_NOPE].reshape(MLA_KV_RANK, -1),
                           ukv[:, :, MLA_NOPE:].reshape(MLA_KV_RANK, -1)], axis=1)
    return uq.astype(BF16), ukv.astype(BF16)


def _tile(n, want):
    t = min(n, want)
    assert n % t == 0, (n, t)
    return t


def kernel(x, p, positions, norm_gains, ffn_w_gate, ffn_w_up, ffn_w_down, w_in, w_out, mix_norm_gain, ssm_a_re, ssm_a_im, ssm_b_re, ssm_b_im, ssm_c_re, ssm_c_im, ssm_d, ssm_log_dt, ssm_w_glu, ssm_b_glu, mlstm_conv_w, mlstm_conv_b, mlstm_b_i, mlstm_b_f, mla_q_norm, mla_kv_norm, mla_w_uq, mla_w_ukv, ple_w_proj, ple_w_gate):
    bsz, seq, d = x.shape
    n = bsz * seq
    depth = p.shape[0]
    assert d == D_MODEL and seq % CHUNK == 0

    tm_ffn = _tile(n, 1024)
    tf_ffn = _tile(D_FF, 512)
    tm_proj = _tile(n, 256)
    tm_row = _tile(n, 512)
    t_seq = _tile(seq, 512)
    t_s5 = _tile(seq, 256)
    t_att = _tile(seq, 1024)

    rope = _rope_table(positions, _tile(n, 2048))
    h = x.reshape(n, d).astype(F32)
    gains = norm_gains.astype(F32).reshape(depth, 8, 1, d)
    mixg = mix_norm_gain.astype(F32).reshape(depth, 4, 1, 512)
    wg_all, wu_all, wd_all = ffn_w_gate.astype(BF16), ffn_w_up.astype(BF16), ffn_w_down.astype(BF16)

    for i in range(depth):
        g = gains[i]
        h = _ffn(h, g[0], g[1], wg_all[i, 0], wu_all[i, 0], wd_all[i, 0], tm_ffn, tf_ffn)

        w_main, w_gt = _permute_w_in(w_in[i])
        z, slab, gt = _inproj(h, g[2], w_main, w_gt, tm_proj)

        wb, tabs, wc = _s5_tables(ssm_a_re[i], ssm_a_im[i], ssm_b_re[i], ssm_b_im[i],
                                  ssm_c_re[i], ssm_c_im[i], ssm_log_dt[i])
        out_a = _s5(z, wb, tabs, wc, ssm_d[i].astype(F32).reshape(1, -1), ssm_w_glu[i].astype(BF16),
                    ssm_b_glu[i].astype(F32).reshape(1, -1), mixg[i, 0], bsz, seq, t_s5)
        out_b = _mlstm(z, slab, gt, mlstm_conv_w[i], mlstm_conv_b[i], mlstm_b_i[i], mlstm_b_f[i],
                       mixg[i, 1], bsz, seq, t_seq)
        wuq, wukv = _permute_mla(mla_w_uq[i], mla_w_ukv[i])
        q, k, v = _mla_proj(z, slab, rope, mla_q_norm[i].astype(F32).reshape(1, -1),
                            mla_kv_norm[i].astype(F32).reshape(1, -1), wuq, wukv, bsz, seq, t_seq)
        out_c = _flash(q, k, v, mixg[i, 2], t_att)
        out_d = _retention(z, rope, mixg[i, 3], bsz, seq, t_seq)

        h = _outproj(h, (out_a, out_b, out_c, out_d), w_out[i].astype(BF16), g[3], tm_row)
        h = _ffn(h, g[4], g[5], wg_all[i, 1], wu_all[i, 1], wd_all[i, 1], tm_ffn, tf_ffn)
        h = _ple(h, p[i].reshape(n, PLE_DIM), g[6], g[7], ple_w_gate[i].astype(BF16),
                 ple_w_proj[i].astype(BF16), tm_row)
    return h.reshape(bsz, seq, d).astype(x.dtype)
```

```python
import functools
import math

import jax
import jax.numpy as jnp
import numpy as np
from jax import lax
from jax.experimental import pallas as pl
from jax.experimental.pallas import tpu as pltpu

F32 = jnp.float32
BF16 = jnp.bfloat16

EPS = 1e-6
ROPE_BASE = 10000.0

D_MODEL = 2048
PLE_DIM = 256
D_FF = 5632

SSM_WIDTH = 512
SSM_GROUP = 16
SSM_GROUPS = SSM_WIDTH // SSM_GROUP
SSM_STATE = 64
SSM_LANES = SSM_GROUPS * SSM_STATE

HEADS = 4
QK = 64
HV = 128
CHUNK = 256
MLSTM_CONV = 4
MLA_Q_RANK = 384
MLA_KV_RANK = 256
MLA_NOPE = 128
MLA_QK = MLA_NOPE + QK

V7X_LANES = 128
V7X_SUBLANES = 8
V7X_VMEM_BYTES = 64 * 1024 * 1024
MIB = 1024 * 1024

Z_SU, Z_MV, Z_MO, Z_RV, Z_RG = 0, 512, 1024, 1536, 2048
Z_MQ, Z_MK, Z_CKV, Z_RQ, Z_RK = 2560, 2816, 3072, 3328, 3584
Z_CQ = 3840
Z_WIDTH = 4224
SLAB_WIDTH = 128
SLAB_I, SLAB_F = 64, 68


def _params(sem, vmem_mib):
    return pltpu.CompilerParams(dimension_semantics=sem, vmem_limit_bytes=int(vmem_mib * MIB))


def _rms(x, gain):
    ms = jnp.mean(x * x, axis=-1, keepdims=True)
    return x * lax.rsqrt(ms + EPS) * gain


def _dot(a, b):
    return jnp.dot(a, b, preferred_element_type=F32)


def _dot_nt(a, b):
    return lax.dot_general(a, b, (((1,), (1,)), ((), ())), preferred_element_type=F32)


def _dot_tn(a, b):
    return lax.dot_general(a, b, (((0,), (0,)), ((), ())), preferred_element_type=F32)


def _split_dot(tri, x, nt):
    hi = x.astype(BF16)
    lo = (x - hi.astype(F32)).astype(BF16)
    if nt:
        return _dot(hi, tri) + _dot(lo, tri)
    return _dot(tri, hi) + _dot(tri, lo)


def _rope_kernel(pos_ref, inv_ref, o_ref):
    ang = pos_ref[...].astype(F32) * inv_ref[...]
    lane = lax.broadcasted_iota(jnp.int32, ang.shape, 1)
    sgn = jnp.where(lane < 96, -1.0, 1.0).astype(F32)
    o_ref[...] = jnp.where(lane < 64, jnp.cos(ang), sgn * jnp.sin(ang))


def _rope_table(positions, tm):
    n = positions.size
    inv = 1.0 / (ROPE_BASE ** (jnp.arange(0, QK, 2, dtype=F32) / QK))
    inv = jnp.tile(inv, 4).reshape(1, 128)
    return pl.pallas_call(
        _rope_kernel,
        grid=(n // tm,),
        in_specs=[pl.BlockSpec((tm, 1), lambda i: (i, 0)),
                  pl.BlockSpec((1, 128), lambda i: (0, 0))],
        out_specs=pl.BlockSpec((tm, 128), lambda i: (i, 0)),
        out_shape=jax.ShapeDtypeStruct((n, 128), F32),
        compiler_params=_params(("parallel",), 32),
        name="rope_table",
    )(positions.reshape(n, 1), inv)


def _apply_rope(x, rope, nheads):
    width = nheads * QK
    cos = rope[:, :QK]
    sin = rope[:, QK:]
    if nheads > 1:
        cos = jnp.concatenate([cos] * nheads, axis=-1)
        sin = jnp.concatenate([sin] * nheads, axis=-1)
    if width >= V7X_LANES:
        lane = lax.broadcasted_iota(jnp.int32, x.shape, 1)
        first = (lane % QK) < (QK // 2)
        rot = jnp.where(first, pltpu.roll(x, width - QK // 2, 1), pltpu.roll(x, QK // 2, 1))
    else:
        rot = jnp.concatenate([x[:, QK // 2:], x[:, :QK // 2]], axis=-1)
    return x * cos + rot * sin


def _ffn_kernel(h_ref, gpre_ref, gpost_ref, wg_ref, wu_ref, wd_ref, o_ref, xn_ref):
    j = pl.program_id(1)

    @pl.when(j == 0)
    def _():
        xn_ref[...] = _rms(h_ref[...], gpre_ref[...]).astype(BF16)
        o_ref[...] = jnp.zeros_like(o_ref)

    x = xn_ref[...]
    g = _dot(x, wg_ref[...])
    u = _dot(x, wu_ref[...])
    a = (g * jax.nn.sigmoid(g) * u).astype(BF16)
    o_ref[...] += _dot(a, wd_ref[...])

    @pl.when(j == pl.num_programs(1) - 1)
    def _():
        o_ref[...] = h_ref[...] + _rms(o_ref[...], gpost_ref[...])


def _ffn(h, gpre, gpost, wg, wu, wd, tm, tf):
    n, d = h.shape
    f = wg.shape[1]
    vmem = (4 * tm * d * 4 + tm * d * 2 + 6 * d * tf * 2 + 6 * tm * tf * 4) / MIB + 4
    return pl.pallas_call(
        _ffn_kernel,
        grid=(n // tm, f // tf),
        in_specs=[pl.BlockSpec((tm, d), lambda i, j: (i, 0), pipeline_mode=pl.Buffered(1)),
                  pl.BlockSpec((1, d), lambda i, j: (0, 0)),
                  pl.BlockSpec((1, d), lambda i, j: (0, 0)),
                  pl.BlockSpec((d, tf), lambda i, j: (0, j)),
                  pl.BlockSpec((d, tf), lambda i, j: (0, j)),
                  pl.BlockSpec((tf, d), lambda i, j: (j, 0))],
        out_specs=pl.BlockSpec((tm, d), lambda i, j: (i, 0)),
        out_shape=jax.ShapeDtypeStruct((n, d), F32),
        scratch_shapes=[pltpu.VMEM((tm, d), BF16)],
        compiler_params=_params(("parallel", "arbitrary"), vmem),
        name="ffn",
    )(h, gpre, 0.5 * gpost, wg, wu, wd)


def _inproj_kernel(h_ref, g_ref, w_ref, wgt_ref, z_ref, slab_ref, gt_ref):
    xn = _rms(h_ref[...], g_ref[...]).astype(BF16)
    z = _dot(xn, w_ref[...])
    z_ref[...] = z[:, :Z_WIDTH].astype(BF16)
    slab_ref[...] = z[:, Z_WIDTH:]
    gt_ref[...] = _dot_nt(wgt_ref[...], xn)


def _inproj(h, gain, w, wgt, tm):
    n, d = h.shape
    wtot = Z_WIDTH + SLAB_WIDTH
    vmem = (2 * tm * d * 4 + d * wtot * 2 + 2 * tm * wtot * 4 + tm * wtot * 4) / MIB + 6
    return pl.pallas_call(
        _inproj_kernel,
        grid=(n // tm,),
        in_specs=[pl.BlockSpec((tm, d), lambda i: (i, 0)),
                  pl.BlockSpec((1, d), lambda i: (0, 0)),
                  pl.BlockSpec((d, wtot), lambda i: (0, 0), pipeline_mode=pl.Buffered(1)),
                  pl.BlockSpec((8, d), lambda i: (0, 0))],
        out_specs=[pl.BlockSpec((tm, Z_WIDTH), lambda i: (i, 0)),
                   pl.BlockSpec((tm, SLAB_WIDTH), lambda i: (i, 0)),
                   pl.BlockSpec((8, tm), lambda i: (0, i))],
        out_shape=[jax.ShapeDtypeStruct((n, Z_WIDTH), BF16),
                   jax.ShapeDtypeStruct((n, SLAB_WIDTH), F32),
                   jax.ShapeDtypeStruct((8, n), F32)],
        compiler_params=_params(("parallel",), vmem),
        name="inproj",
    )(h, gain, w, wgt)


SCAN_COLS = 512


def _s5_kernel(u_ref, wb_ref, tab_ref, wc_ref, d_ref, wglu_ref, bglu_ref, gain_ref,
               o_ref, x_ref, carry_ref):
    t = u_ref.shape[0]
    nl = SSM_LANES

    @pl.when(pl.program_id(1) == 0)
    def _():
        carry_ref[...] = jnp.zeros_like(carry_ref)

    u = u_ref[...]
    x_ref[...] = _dot(u, wb_ref[...])

    for c0 in range(0, nl, SCAN_COLS):
        re = slice(c0, c0 + SCAN_COLS)
        im = slice(nl + c0, nl + c0 + SCAN_COLS)
        tabs = [(tab_ref[2 * k, :, re], tab_ref[2 * k + 1, :, re]) for k in range(4)]

        def block(i, carry, re=re, im=im, tabs=tabs):
            cr, ci = carry
            r0 = pl.multiple_of(i * V7X_SUBLANES, V7X_SUBLANES)
            xr = x_ref[pl.ds(r0, V7X_SUBLANES), re]
            xi = x_ref[pl.ds(r0, V7X_SUBLANES), im]
            for k, shift in enumerate((1, 2, 4)):
                ar, ai = tabs[k]
                rr = pltpu.roll(xr, shift, 0)
                ri = pltpu.roll(xi, shift, 0)
                xr, xi = xr + ar * rr - ai * ri, xi + ar * ri + ai * rr
            pr, pi = tabs[3]
            xr, xi = xr + pr * cr - pi * ci, xi + pr * ci + pi * cr
            x_ref[pl.ds(r0, V7X_SUBLANES), re] = xr
            x_ref[pl.ds(r0, V7X_SUBLANES), im] = xi
            last = V7X_SUBLANES - 1
            return (jnp.broadcast_to(xr[last:, :], xr.shape), jnp.broadcast_to(xi[last:, :], xi.shape))

        cr0 = carry_ref[:, re]
        ci0 = carry_ref[:, im]
        cr1, ci1 = lax.fori_loop(0, t // V7X_SUBLANES, block, (cr0, ci0))
        carry_ref[:, re] = cr1
        carry_ref[:, im] = ci1

    y = _dot(x_ref[...].astype(BF16), wc_ref[...]) + d_ref[...] * u.astype(F32)
    g = jax.nn.gelu(y)
    out = g * jax.nn.sigmoid(_dot(g.astype(BF16), wglu_ref[...]) + bglu_ref[...])
    o_ref[...] = _rms(out, gain_ref[...]).astype(o_ref.dtype)


def _s5(z, wb, tabs, wc, d_skip, wglu, bglu, gain, bsz, seq, t):
    n = bsz * seq
    nt = seq // t
    nl2 = 2 * SSM_LANES
    const = lambda b, i: (0, 0)
    vmem = (t * nl2 * 4 * 3 + 2 * SSM_WIDTH * nl2 * 2 * 2 + 8 * 8 * SSM_LANES * 4 * 2) / MIB + 8
    return pl.pallas_call(
        _s5_kernel,
        grid=(bsz, nt),
        in_specs=[pl.BlockSpec((t, SSM_WIDTH), lambda b, i: (b * nt + i, Z_SU // SSM_WIDTH)),
                  pl.BlockSpec((SSM_WIDTH, nl2), const),
                  pl.BlockSpec((8, V7X_SUBLANES, SSM_LANES), lambda b, i: (0, 0, 0)),
                  pl.BlockSpec((nl2, SSM_WIDTH), const),
                  pl.BlockSpec((1, SSM_WIDTH), const),
                  pl.BlockSpec((SSM_WIDTH, SSM_WIDTH), const),
                  pl.BlockSpec((1, SSM_WIDTH), const),
                  pl.BlockSpec((1, SSM_WIDTH), const)],
        out_specs=pl.BlockSpec((t, SSM_WIDTH), lambda b, i: (b * nt + i, 0)),
        out_shape=jax.ShapeDtypeStruct((n, SSM_WIDTH), BF16),
        scratch_shapes=[pltpu.VMEM((t, nl2), F32), pltpu.VMEM((V7X_SUBLANES, nl2), F32)],
        compiler_params=_params(("arbitrary", "arbitrary"), vmem),
        name="s5",
    )(z, wb, tabs, wc, d_skip, wglu, bglu, gain)


def _s5_tables(a_re, a_im, b_re, b_im, c_re, c_im, log_dt):
    lr, li = a_re.astype(F32), a_im.astype(F32)
    dt = jnp.exp(log_dt.astype(F32))[:, None]
    mag = jnp.exp(lr * dt)
    ar, ai = mag * jnp.cos(li * dt), mag * jnp.sin(li * dt)
    den = lr * lr + li * li
    nr, ni = ar - 1.0, ai
    zr = (nr * lr + ni * li) / den
    zi = (ni * lr - nr * li) / den
    br_, bi_ = b_re.astype(F32), b_im.astype(F32)
    bbr = zr[..., None] * br_ - zi[..., None] * bi_
    bbi = zr[..., None] * bi_ + zi[..., None] * br_
    eye = jnp.eye(SSM_GROUPS, dtype=F32)
    wb_re = jnp.einsum('gnc,gh->gchn', bbr, eye).reshape(SSM_WIDTH, SSM_LANES)
    wb_im = jnp.einsum('gnc,gh->gchn', bbi, eye).reshape(SSM_WIDTH, SSM_LANES)
    wb = jnp.concatenate([wb_re, wb_im], axis=1).astype(BF16)
    wc_re = jnp.einsum('gcn,gh->gnhc', c_re.astype(F32), eye).reshape(SSM_LANES, SSM_WIDTH)
    wc_im = jnp.einsum('gcn,gh->gnhc', c_im.astype(F32), eye).reshape(SSM_LANES, SSM_WIDTH)
    wc = jnp.concatenate([wc_re, -wc_im], axis=0).astype(BF16)

    a1 = (ar.reshape(1, SSM_LANES), ai.reshape(1, SSM_LANES))

    def cmul(p, q):
        return (p[0] * q[0] - p[1] * q[1], p[0] * q[1] + p[1] * q[0])

    pows = [a1]
    for _ in range(7):
        pows.append(cmul(pows[-1], a1))
    row = jnp.arange(V7X_SUBLANES)[:, None]
    tabs = []
    for shift in (1, 2, 4):
        pr, pi = pows[shift - 1]
        keep = row >= shift
        tabs += [jnp.where(keep, pr, 0.0), jnp.where(keep, pi, 0.0)]
    tabs += [jnp.concatenate([p[0] for p in pows], axis=0), jnp.concatenate([p[1] for p in pows], axis=0)]
    return wb, jnp.stack(tabs).astype(F32), wc


def _ret_kernel(q_ref, k_ref, v_ref, g_ref, rope_ref, dmask_ref, qdec_ref, kdec_ref, cdec_ref, gain_ref,
                o_ref, state_ref):
    t = q_ref.shape[0]

    @pl.when(pl.program_id(1) == 0)
    def _():
        state_ref[...] = jnp.zeros_like(state_ref)

    for c in range(t // CHUNK):
        rows = slice(c * CHUNK, (c + 1) * CHUNK)
        rope = rope_ref[rows, :]
        q = _apply_rope(q_ref[rows, :].astype(F32), rope, HEADS)
        k = _apply_rope(k_ref[rows, :].astype(F32), rope, HEADS) * (QK ** -0.5)
        for h in range(HEADS):
            qh = q[:, h * QK:(h + 1) * QK].astype(BF16)
            kh = k[:, h * QK:(h + 1) * QK]
            vh = v_ref[rows, h * HV:(h + 1) * HV]
            s = _dot_nt(qh, kh.astype(BF16)) * dmask_ref[h]
            inner = _dot(s.astype(BF16), vh)
            r_old = state_ref[h]
            cross = _dot(qh, r_old.astype(BF16)) * qdec_ref[h]
            state_ref[h] = cdec_ref[h] * r_old + _dot_tn((kh * kdec_ref[h]).astype(BF16), vh)
            y = _rms(inner + cross, gain_ref[:, h * HV:(h + 1) * HV])
            gate = g_ref[rows, h * HV:(h + 1) * HV].astype(F32)
            o_ref[rows, h * HV:(h + 1) * HV] = (gate * jax.nn.sigmoid(gate) * y).astype(o_ref.dtype)


def _ret_tables():
    log_g = jnp.log1p(-jnp.exp2(-5.0 - jnp.arange(HEADS, dtype=F32)))
    idx = jnp.arange(CHUNK)
    diff = idx[:, None] - idx[None, :]
    causal = diff >= 0
    dmask = jnp.where(causal, jnp.exp(jnp.where(causal, diff, 0).astype(F32) * log_g[:, None, None]), 0.0)
    q_decay = jnp.exp((idx + 1).astype(F32)[None, :] * log_g[:, None])
    k_decay = jnp.exp((CHUNK - 1 - idx).astype(F32)[None, :] * log_g[:, None])
    chunk_decay = jnp.exp(CHUNK * log_g)
    qdec = jnp.broadcast_to(q_decay[:, :, None], (HEADS, CHUNK, HV))
    kdec = jnp.broadcast_to(k_decay[:, :, None], (HEADS, CHUNK, QK))
    cdec = jnp.broadcast_to(chunk_decay[:, None, None], (HEADS, QK, HV))
    return dmask, qdec, kdec, cdec


def _retention(z, rope, gain, bsz, seq, t):
    n = bsz * seq
    nt = seq // t
    dmask, qdec, kdec, cdec = _ret_tables()
    c3 = lambda b, i: (0, 0, 0)

    def col(width, lane0):
        return pl.BlockSpec((t, width), lambda b, i: (b * nt + i, lane0 // width))

    return pl.pallas_call(
        _ret_kernel,
        grid=(bsz, nt),
        in_specs=[col(256, Z_RQ), col(256, Z_RK), col(512, Z_RV), col(512, Z_RG),
                  pl.BlockSpec((t, 128), lambda b, i: (b * nt + i, 0)),
                  pl.BlockSpec((HEADS, CHUNK, CHUNK), c3),
                  pl.BlockSpec((HEADS, CHUNK, HV), c3),
                  pl.BlockSpec((HEADS, CHUNK, QK), c3),
                  pl.BlockSpec((HEADS, QK, HV), c3),
                  pl.BlockSpec((1, HEADS * HV), lambda b, i: (0, 0))],
        out_specs=pl.BlockSpec((t, HEADS * HV), lambda b, i: (b * nt + i, 0)),
        out_shape=jax.ShapeDtypeStruct((n, HEADS * HV), BF16),
        scratch_shapes=[pltpu.VMEM((HEADS, QK, HV), F32)],
        compiler_params=_params(("arbitrary", "arbitrary"), 32),
        name="retention",
    )(z, z, z, z, rope, dmask, qdec, kdec, cdec, gain)


TAIL = V7X_SUBLANES


def _mlstm_kernel(q_ref, k_ref, v_ref, og_ref, slab_ref, gt_ref, cw_ref, cb_ref, brow_ref, bcol_ref,
                  tril_ref, triu_ref, gain_ref, o_ref, xbuf_ref, c_ref, m_ref):
    t = q_ref.shape[0]
    width = 2 * HEADS * QK

    @pl.when(pl.program_id(1) == 0)
    def _():
        xbuf_ref[0:TAIL, :] = jnp.zeros((TAIL, width), F32)
        c_ref[...] = jnp.zeros_like(c_ref)
        m_ref[...] = jnp.zeros_like(m_ref)

    xbuf_ref[TAIL:, 0:HEADS * QK] = q_ref[...].astype(F32)
    xbuf_ref[TAIL:, HEADS * QK:] = k_ref[...].astype(F32)
    acc = cb_ref[...] + cw_ref[MLSTM_CONV - 1:MLSTM_CONV, :] * xbuf_ref[TAIL:, :]
    for j in range(MLSTM_CONV - 1):
        off = TAIL - (MLSTM_CONV - 1) + j
        acc = acc + cw_ref[j:j + 1, :] * xbuf_ref[off:off + t, :]
    xbuf_ref[0:TAIL, :] = xbuf_ref[t:t + TAIL, :]
    qk = acc * jax.nn.sigmoid(acc)

    gr = gt_ref[...] + brow_ref[...]
    is_f_row = lax.broadcasted_iota(jnp.int32, gr.shape, 0) >= HEADS
    lf_r = jnp.where(is_f_row, jax.nn.log_sigmoid(gr), 0.0)
    gc = slab_ref[...] + bcol_ref[...]
    lane = lax.broadcasted_iota(jnp.int32, gc.shape, 1)
    is_f_col = (lane >= SLAB_F) & (lane < SLAB_F + HEADS)
    lf_c = jnp.where(is_f_col, jax.nn.log_sigmoid(gc), 0.0)

    row = lax.broadcasted_iota(jnp.int32, (CHUNK, CHUNK), 0)
    colid = lax.broadcasted_iota(jnp.int32, (CHUNK, CHUNK), 1)
    causal = colid <= row
    ones_col = (lax.broadcasted_iota(jnp.int32, (CHUNK, HV), 1) == 0).astype(BF16)

    for c in range(t // CHUNK):
        rows = slice(c * CHUNK, (c + 1) * CHUNK)
        bh_r = _split_dot(triu_ref[...], lf_r[:, rows], nt=True)
        bh_c = _split_dot(tril_ref[...], lf_c[rows, :], nt=False)
        a_r = gr[0:HEADS, rows] - bh_r[HEADS:, :]
        a_c = gc[rows, SLAB_I:SLAB_I + HEADS] - bh_c[:, SLAB_F:SLAB_F + HEADS]
        bh_c = bh_c[:, SLAB_F:SLAB_F + HEADS]
        for h in range(HEADS):
            qh = (qk[rows, h * QK:(h + 1) * QK] * (QK ** -0.5)).astype(BF16)
            kh = qk[rows, (HEADS + h) * QK:(HEADS + h + 1) * QK]
            vaug = jnp.concatenate([v_ref[rows, h * HV:(h + 1) * HV], ones_col], axis=-1)
            m_old = m_ref[h][0:1, 0:1]
            amat = jnp.where(causal, a_r[h:h + 1, :], -jnp.inf)
            big_m = jnp.maximum(jnp.max(amat, axis=-1, keepdims=True), m_old)
            dmat = jnp.exp(amat - big_m)
            s = _dot_nt(qh, kh.astype(BF16)) * dmat
            sc = jnp.exp(m_old - big_m)
            c_old = c_ref[h]
            tot = _dot(s.astype(BF16), vaug) + sc * _dot(qh, c_old.astype(BF16))
            num = tot[:, :HV]
            den = tot[:, HV:HV + 1]
            m_row = bh_c[:, h:h + 1] + big_m
            den = jnp.maximum(jnp.abs(den), jnp.exp(-m_row))
            hout = num / den
            m_last = big_m[CHUNK - 1:CHUNK, :]
            b_last = bh_c[CHUNK - 1:CHUNK, h:h + 1]
            wk = jnp.exp(a_c[:, h:h + 1] - m_last)
            dec = jnp.exp(m_old - m_last)
            c_ref[h] = dec * c_old + _dot_tn((kh * wk).astype(BF16), vaug)
            m_ref[h] = jnp.broadcast_to(b_last + m_last, m_ref.shape[1:])
            y = _rms(hout, gain_ref[:, h * HV:(h + 1) * HV])
            og = og_ref[rows, h * HV:(h + 1) * HV].astype(F32)
            o_ref[rows, h * HV:(h + 1) * HV] = (jax.nn.sigmoid(og) * y).astype(o_ref.dtype)


def _mlstm(z, slab, gt, conv_w, conv_b, b_i, b_f, gain, bsz, seq, t):
    n = bsz * seq
    nt = seq // t
    idx = jnp.arange(CHUNK)
    tril = (idx[None, :] <= idx[:, None]).astype(BF16)
    triu = tril.T
    brow = jnp.concatenate([b_i, b_f]).astype(F32).reshape(2 * HEADS, 1)
    bcol = jnp.zeros((1, SLAB_WIDTH), F32).at[0, SLAB_I:SLAB_I + 2 * HEADS].set(brow[:, 0])
    c2 = lambda b, i: (0, 0)

    def col(width, lane0):
        return pl.BlockSpec((t, width), lambda b, i: (b * nt + i, lane0 // width))

    return pl.pallas_call(
        _mlstm_kernel,
        grid=(bsz, nt),
        in_specs=[col(256, Z_MQ), col(256, Z_MK), col(512, Z_MV), col(512, Z_MO),
                  pl.BlockSpec((t, SLAB_WIDTH), lambda b, i: (b * nt + i, 0)),
                  pl.BlockSpec((2 * HEADS, t), lambda b, i: (0, b * nt + i)),
                  pl.BlockSpec((MLSTM_CONV, 2 * HEADS * QK), c2),
                  pl.BlockSpec((1, 2 * HEADS * QK), c2),
                  pl.BlockSpec((2 * HEADS, 1), c2),
                  pl.BlockSpec((1, SLAB_WIDTH), c2),
                  pl.BlockSpec((CHUNK, CHUNK), c2),
                  pl.BlockSpec((CHUNK, CHUNK), c2),
                  pl.BlockSpec((1, HEADS * HV), c2)],
        out_specs=pl.BlockSpec((t, HEADS * HV), lambda b, i: (b * nt + i, 0)),
        out_shape=jax.ShapeDtypeStruct((n, HEADS * HV), BF16),
        scratch_shapes=[pltpu.VMEM((t + TAIL, 2 * HEADS * QK), F32),
                        pltpu.VMEM((HEADS, QK, 2 * HV), F32),
                        pltpu.VMEM((HEADS, V7X_SUBLANES, V7X_LANES), F32)],
        compiler_params=_params(("arbitrary", "arbitrary"), 32),
        name="mlstm",
    )(z, z, z, z, slab, gt, conv_w.astype(F32), conv_b.astype(F32).reshape(1, -1), brow, bcol, tril, triu, gain)


VT_ROWS = HV + 16


def _mla_proj_kernel(cq_ref, ckv_ref, slab_ref, rope_ref, qn_ref, kvn_ref, wuq_ref, wuk_ref, wuvt_ref,
                     q_ref, k_ref, vt_ref):
    rope = rope_ref[...]
    cq = _rms(cq_ref[...].astype(F32), qn_ref[...]).astype(BF16)
    ckv = _rms(ckv_ref[...].astype(F32), kvn_ref[...]).astype(BF16)
    q = _dot(cq, wuq_ref[...])
    kn = _dot(ckv, wuk_ref[...])
    vt = _dot_nt(wuvt_ref[...], ckv)
    scale = (MLA_QK ** -0.5) * math.log2(math.e)
    q_rope = _apply_rope(q[:, HEADS * MLA_NOPE:], rope, HEADS)
    k_rope = _apply_rope(slab_ref[:, 0:QK], rope, 1)
    pad = VT_ROWS - HV
    ones_row = (lax.broadcasted_iota(jnp.int32, (pad, vt.shape[1]), 0) == 0).astype(BF16)
    for h in range(HEADS):
        q_ref[0, h, :, 0:MLA_NOPE] = (q[:, h * MLA_NOPE:(h + 1) * MLA_NOPE] * scale).astype(BF16)
        q_ref[0, h, :, MLA_NOPE:] = (q_rope[:, h * QK:(h + 1) * QK] * scale).astype(BF16)
        k_ref[0, h, :, 0:MLA_NOPE] = kn[:, h * MLA_NOPE:(h + 1) * MLA_NOPE].astype(BF16)
        k_ref[0, h, :, MLA_NOPE:] = k_rope.astype(BF16)
        vt_ref[0, h, 0:HV, :] = vt[h * HV:(h + 1) * HV, :].astype(BF16)
        vt_ref[0, h, HV:, :] = ones_row


def _mla_proj(z, slab, rope, q_norm, kv_norm, wuq, wuk, wuvt, bsz, seq, t):
    nt = seq // t
    c2 = lambda b, i: (0, 0)
    return pl.pallas_call(
        _mla_proj_kernel,
        grid=(bsz, nt),
        in_specs=[pl.BlockSpec((t, MLA_Q_RANK), lambda b, i: (b * nt + i, Z_CQ // MLA_Q_RANK)),
                  pl.BlockSpec((t, MLA_KV_RANK), lambda b, i: (b * nt + i, Z_CKV // MLA_KV_RANK)),
                  pl.BlockSpec((t, SLAB_WIDTH), lambda b, i: (b * nt + i, 0)),
                  pl.BlockSpec((t, 128), lambda b, i: (b * nt + i, 0)),
                  pl.BlockSpec((1, MLA_Q_RANK), c2),
                  pl.BlockSpec((1, MLA_KV_RANK), c2),
                  pl.BlockSpec((MLA_Q_RANK, HEADS * MLA_QK), c2),
                  pl.BlockSpec((MLA_KV_RANK, HEADS * MLA_NOPE), c2),
                  pl.BlockSpec((HEADS * HV, MLA_KV_RANK), c2)],
        out_specs=[pl.BlockSpec((1, HEADS, t, MLA_QK), lambda b, i: (b, 0, i, 0)),
                   pl.BlockSpec((1, HEADS, t, MLA_QK), lambda b, i: (b, 0, i, 0)),
                   pl.BlockSpec((1, HEADS, VT_ROWS, t), lambda b, i: (b, 0, 0, i))],
        out_shape=[jax.ShapeDtypeStruct((bsz, HEADS, seq, MLA_QK), BF16),
                   jax.ShapeDtypeStruct((bsz, HEADS, seq, MLA_QK), BF16),
                   jax.ShapeDtypeStruct((bsz, HEADS, VT_ROWS, seq), BF16)],
        compiler_params=_params(("parallel", "parallel"), 40),
        name="mla_proj",
    )(z, z, slab, rope, q_norm, kv_norm, wuq, wuk, wuvt)


FLASH_SUB = 1024


def _flash_kernel(q_ref, k_ref, vt_ref, gain_ref, o_ref, m_ref, acc_ref):
    qi = pl.program_id(1)
    ki = pl.program_id(2)
    tk = k_ref.shape[2]
    sub = min(FLASH_SUB, tk)

    @pl.when(ki == 0)
    def _():
        m_ref[...] = jnp.full_like(m_ref, -jnp.inf)
        acc_ref[...] = jnp.zeros_like(acc_ref)

    def step(diagonal):
        for h in range(HEADS):
            for c in range(tk // sub):
                c0 = c * sub if diagonal else 0
                keys = slice(c * sub, (c + 1) * sub)
                st = _dot_nt(k_ref[0, h, keys, :], q_ref[0, h, c0:, :])
                if diagonal:
                    key = c * sub + lax.broadcasted_iota(jnp.int32, st.shape, 0)
                    qry = c0 + lax.broadcasted_iota(jnp.int32, st.shape, 1)
                    st = jnp.where(key <= qry, st, -jnp.inf)
                m_old = m_ref[h, 0:1, c0:]
                m_new = jnp.maximum(m_old, jnp.max(st, axis=0, keepdims=True))
                alpha = jnp.exp2(m_old - m_new)
                pt = jnp.exp2(st - m_new).astype(BF16)
                acc_ref[h, :, c0:] = alpha * acc_ref[h, :, c0:] + _dot(vt_ref[0, h, :, keys], pt)
                m_ref[h, :, c0:] = jnp.broadcast_to(m_new, (V7X_SUBLANES, m_new.shape[1]))

    @pl.when(ki < qi)
    def _():
        step(False)

    @pl.when(ki == qi)
    def _():
        step(True)
        for h in range(HEADS):
            acc = acc_ref[h]
            yt = acc[0:HV, :] / acc[HV:HV + 1, :]
            yt = yt * lax.rsqrt(jnp.mean(yt * yt, axis=0, keepdims=True) + EPS)
            o_ref[:, h * HV:(h + 1) * HV] = (yt.T * gain_ref[:, h * HV:(h + 1) * HV]).astype(o_ref.dtype)


def _flash(q, k, vt, gain, t):
    bsz, _, seq, _ = q.shape
    nt = seq // t
    return pl.pallas_call(
        _flash_kernel,
        grid=(bsz, nt, nt),
        in_specs=[pl.BlockSpec((1, HEADS, t, MLA_QK), lambda b, i, j: (b, 0, i, 0)),
                  pl.BlockSpec((1, HEADS, t, MLA_QK), lambda b, i, j: (b, 0, jnp.minimum(i, j), 0)),
                  pl.BlockSpec((1, HEADS, VT_ROWS, t), lambda b, i, j: (b, 0, 0, jnp.minimum(i, j))),
                  pl.BlockSpec((1, HEADS * HV), lambda b, i, j: (0, 0))],
        out_specs=pl.BlockSpec((t, HEADS * HV), lambda b, i, j: (b * nt + i, 0)),
        out_shape=jax.ShapeDtypeStruct((bsz * seq, HEADS * HV), BF16),
        scratch_shapes=[pltpu.VMEM((HEADS, V7X_SUBLANES, t), F32),
                        pltpu.VMEM((HEADS, VT_ROWS, t), F32)],
        compiler_params=_params(("parallel", "parallel", "arbitrary"), 56),
        name="mla_flash",
    )(q, k, vt, gain)


def _outproj_kernel(h_ref, a_ref, b_ref, c_ref, d_ref, w_ref, g_ref, o_ref):
    y = _dot(a_ref[...], w_ref[0:512, :])
    y += _dot(b_ref[...], w_ref[512:1024, :])
    y += _dot(c_ref[...], w_ref[1024:1536, :])
    y += _dot(d_ref[...], w_ref[1536:2048, :])
    o_ref[...] = h_ref[...] + _rms(y, g_ref[...])


def _outproj(h, mixes, w, gain, tm):
    n, d = h.shape
    row = lambda i: (i, 0)
    vmem = (4 * tm * d * 4 + d * d * 2 + 8 * tm * 512 * 2 + 2 * tm * d * 4) / MIB + 6
    return pl.pallas_call(
        _outproj_kernel,
        grid=(n // tm,),
        in_specs=[pl.BlockSpec((tm, d), row)] + [pl.BlockSpec((tm, 512), row)] * 4 +
                 [pl.BlockSpec((d, d), lambda i: (0, 0), pipeline_mode=pl.Buffered(1)),
                  pl.BlockSpec((1, d), lambda i: (0, 0))],
        out_specs=pl.BlockSpec((tm, d), row),
        out_shape=jax.ShapeDtypeStruct((n, d), F32),
        compiler_params=_params(("parallel",), vmem),
        name="outproj",
    )(h, *mixes, w, gain)


def _ple_kernel(h_ref, p_ref, g6_ref, g7_ref, wg_ref, wp_ref, o_ref):
    h = h_ref[...]
    gate = jax.nn.sigmoid(_dot(_rms(h, g6_ref[...]).astype(BF16), wg_ref[...]))
    emb = _rms(_dot(p_ref[...].astype(BF16), wp_ref[...]), g7_ref[...])
    o_ref[...] = h + gate * emb


def _ple(h, p, g6, g7, wg, wp, tm):
    n, d = h.shape
    row = lambda i: (i, 0)
    c2 = lambda i: (0, 0)
    vmem = (4 * tm * d * 4 + d * d * 2 + 2 * PLE_DIM * d * 2 + 3 * tm * d * 4) / MIB + 6
    return pl.pallas_call(
        _ple_kernel,
        grid=(n // tm,),
        in_specs=[pl.BlockSpec((tm, d), row), pl.BlockSpec((tm, PLE_DIM), row),
                  pl.BlockSpec((1, d), c2), pl.BlockSpec((1, d), c2),
                  pl.BlockSpec((d, d), c2, pipeline_mode=pl.Buffered(1)),
                  pl.BlockSpec((PLE_DIM, d), c2)],
        out_specs=pl.BlockSpec((tm, d), row),
        out_shape=jax.ShapeDtypeStruct((n, d), F32),
        compiler_params=_params(("parallel",), vmem),
        name="ple",
    )(h, p, g6, g7, wg, wp)


def _permute_w_in(w_in):
    sizes = (SSM_WIDTH, 256, 256, 512, 512, 4, 4, MLA_Q_RANK, MLA_KV_RANK, QK, 256, 256, 512, 512)
    offs = np.concatenate([[0], np.cumsum(sizes)])
    (s_u, m_q, m_k, m_v, m_o, m_i, m_f, a_cq, a_ckv, a_kr, r_q, r_k, r_v, r_g) = [
        w_in[:, int(offs[i]):int(offs[i + 1])] for i in range(len(sizes))]
    pad = jnp.zeros((w_in.shape[0], SLAB_WIDTH - QK - 2 * HEADS), w_in.dtype)
    main = jnp.concatenate([s_u, m_v, m_o, r_v, r_g, m_q, m_k, a_ckv, r_q, r_k, a_cq,
                            a_kr, m_i, m_f, pad], axis=1).astype(BF16)
    gates_t = jnp.concatenate([m_i, m_f], axis=1).T.astype(BF16)
    return main, gates_t


def _permute_mla(w_uq, w_ukv):
    uq = w_uq.reshape(MLA_Q_RANK, HEADS, MLA_QK)
    uq = jnp.concatenate([uq[:, :, :MLA_NOPE].reshape(MLA_Q_RANK, -1),
                          uq[:, :, MLA_NOPE:].reshape(MLA_Q_RANK, -1)], axis=1)
    ukv = w_ukv.reshape(MLA_KV_RANK, HEADS, MLA_NOPE + HV)
    uk = ukv[:, :, :MLA_NOPE].reshape(MLA_KV_RANK, -1)
    uvt = ukv[:, :, MLA_NOPE:].reshape(MLA_KV_RANK, -1).T
    return uq.astype(BF16), uk.astype(BF16), uvt.astype(BF16)


def _tile(n, want):
    t = min(n, want)
    assert n % t == 0, (n, t)
    return t


def kernel(x, p, positions, norm_gains, ffn_w_gate, ffn_w_up, ffn_w_down, w_in, w_out, mix_norm_gain, ssm_a_re, ssm_a_im, ssm_b_re, ssm_b_im, ssm_c_re, ssm_c_im, ssm_d, ssm_log_dt, ssm_w_glu, ssm_b_glu, mlstm_conv_w, mlstm_conv_b, mlstm_b_i, mlstm_b_f, mla_q_norm, mla_kv_norm, mla_w_uq, mla_w_ukv, ple_w_proj, ple_w_gate):
    bsz, seq, d = x.shape
    n = bsz * seq
    depth = p.shape[0]
    assert d == D_MODEL and seq % CHUNK == 0

    tm_ffn = _tile(n, 1024)
    tf_ffn = _tile(D_FF, 512)
    tm_proj = _tile(n, 256)
    tm_row = _tile(n, 512)
    t_seq = _tile(seq, 512)
    t_s5 = _tile(seq, 256)
    t_att = _tile(seq, 2048)

    rope = _rope_table(positions, _tile(n, 2048))
    h = x.reshape(n, d).astype(F32)
    gains = norm_gains.astype(F32).reshape(depth, 8, 1, d)
    mixg = mix_norm_gain.astype(F32).reshape(depth, 4, 1, 512)
    wg_all, wu_all, wd_all = ffn_w_gate.astype(BF16), ffn_w_up.astype(BF16), ffn_w_down.astype(BF16)

    for i in range(depth):
        g = gains[i]
        h = _ffn(h, g[0], g[1], wg_all[i, 0], wu_all[i, 0], wd_all[i, 0], tm_ffn, tf_ffn)

        w_main, w_gt = _permute_w_in(w_in[i])
        z, slab, gt = _inproj(h, g[2], w_main, w_gt, tm_proj)

        wb, tabs, wc = _s5_tables(ssm_a_re[i], ssm_a_im[i], ssm_b_re[i], ssm_b_im[i],
                                  ssm_c_re[i], ssm_c_im[i], ssm_log_dt[i])
        out_a = _s5(z, wb, tabs, wc, ssm_d[i].astype(F32).reshape(1, -1), ssm_w_glu[i].astype(BF16),
                    ssm_b_glu[i].astype(F32).reshape(1, -1), mixg[i, 0], bsz, seq, t_s5)
        out_b = _mlstm(z, slab, gt, mlstm_conv_w[i], mlstm_conv_b[i], mlstm_b_i[i], mlstm_b_f[i],
                       mixg[i, 1], bsz, seq, t_seq)
        wuq, wuk, wuvt = _permute_mla(mla_w_uq[i], mla_w_ukv[i])
        q, k, vt = _mla_proj(z, slab, rope, mla_q_norm[i].astype(F32).reshape(1, -1),
                             mla_kv_norm[i].astype(F32).reshape(1, -1), wuq, wuk, wuvt, bsz, seq, t_seq)
        out_c = _flash(q, k, vt, mixg[i, 2], t_att)
        out_d = _retention(z, rope, mixg[i, 3], bsz, seq, t_seq)

        h = _outproj(h, (out_a, out_b, out_c, out_d), w_out[i].astype(BF16), g[3], tm_row)
        h = _ffn(h, g[4], g[5], wg_all[i, 1], wu_all[i, 1], wd_all[i, 1], tm_ffn, tf_ffn)
        h = _ple(h, p[i].reshape(n, PLE_DIM), g[6], g[7], ple_w_gate[i].astype(BF16),
                 ple_w_proj[i].astype(BF16), tm_row)
    return h.reshape(bsz, seq, d).astype(x.dtype)
```

```python
import functools
import math

import jax
import jax.numpy as jnp
import numpy as np
from jax import lax
from jax.experimental import pallas as pl
from jax.experimental.pallas import tpu as pltpu

F32 = jnp.float32
BF16 = jnp.bfloat16

EPS = 1e-6
ROPE_BASE = 10000.0

D_MODEL = 2048
PLE_DIM = 256
D_FF = 5632

SSM_WIDTH = 512
SSM_GROUP = 16
SSM_GROUPS = SSM_WIDTH // SSM_GROUP
SSM_STATE = 64
SSM_LANES = SSM_GROUPS * SSM_STATE

HEADS = 4
QK = 64
HV = 128
CHUNK = 256
MLSTM_CONV = 4
MLA_Q_RANK = 384
MLA_KV_RANK = 256
MLA_NOPE = 128
MLA_QK = MLA_NOPE + QK

V7X_LANES = 128
V7X_SUBLANES = 8
V7X_VMEM_BYTES = 64 * 1024 * 1024
MIB = 1024 * 1024

Z_SU, Z_MV, Z_MO, Z_RV, Z_RG = 0, 512, 1024, 1536, 2048
Z_MQ, Z_MK, Z_CKV, Z_RQ, Z_RK = 2560, 2816, 3072, 3328, 3584
Z_CQ = 3840
Z_WIDTH = 4224
SLAB_WIDTH = 128
SLAB_I, SLAB_F = 64, 68


def _params(sem, vmem_mib):
    return pltpu.CompilerParams(dimension_semantics=sem, vmem_limit_bytes=int(vmem_mib * MIB))


def _rms(x, gain):
    ms = jnp.mean(x * x, axis=-1, keepdims=True)
    return x * lax.rsqrt(ms + EPS) * gain


def _dot(a, b):
    return jnp.dot(a, b, preferred_element_type=F32)


def _dot_nt(a, b):
    return lax.dot_general(a, b, (((1,), (1,)), ((), ())), preferred_element_type=F32)


def _dot_tn(a, b):
    return lax.dot_general(a, b, (((0,), (0,)), ((), ())), preferred_element_type=F32)


def _split_dot(tri, x, nt):
    hi = x.astype(BF16)
    lo = (x - hi.astype(F32)).astype(BF16)
    if nt:
        return _dot(hi, tri) + _dot(lo, tri)
    return _dot(tri, hi) + _dot(tri, lo)


def _rope_kernel(pos_ref, inv_ref, o_ref):
    ang = pos_ref[...].astype(F32) * inv_ref[...]
    lane = lax.broadcasted_iota(jnp.int32, ang.shape, 1)
    sgn = jnp.where(lane < 96, -1.0, 1.0).astype(F32)
    o_ref[...] = jnp.where(lane < 64, jnp.cos(ang), sgn * jnp.sin(ang))


def _rope_table(positions, tm):
    n = positions.size
    inv = 1.0 / (ROPE_BASE ** (jnp.arange(0, QK, 2, dtype=F32) / QK))
    inv = jnp.tile(inv, 4).reshape(1, 128)
    return pl.pallas_call(
        _rope_kernel,
        grid=(n // tm,),
        in_specs=[pl.BlockSpec((tm, 1), lambda i: (i, 0)),
                  pl.BlockSpec((1, 128), lambda i: (0, 0))],
        out_specs=pl.BlockSpec((tm, 128), lambda i: (i, 0)),
        out_shape=jax.ShapeDtypeStruct((n, 128), F32),
        compiler_params=_params(("parallel",), 32),
        name="rope_table",
    )(positions.reshape(n, 1), inv)


def _apply_rope(x, rope, nheads):
    width = nheads * QK
    cos = rope[:, :QK]
    sin = rope[:, QK:]
    if nheads > 1:
        cos = jnp.concatenate([cos] * nheads, axis=-1)
        sin = jnp.concatenate([sin] * nheads, axis=-1)
    if width >= V7X_LANES:
        lane = lax.broadcasted_iota(jnp.int32, x.shape, 1)
        first = (lane % QK) < (QK // 2)
        rot = jnp.where(first, pltpu.roll(x, width - QK // 2, 1), pltpu.roll(x, QK // 2, 1))
    else:
        rot = jnp.concatenate([x[:, QK // 2:], x[:, :QK // 2]], axis=-1)
    return x * cos + rot * sin


def _ffn_kernel(h_ref, gpre_ref, gpost_ref, wgu_ref, wd_ref, o_ref, xn_ref):
    j = pl.program_id(1)
    tf = wd_ref.shape[0]

    @pl.when(j == 0)
    def _():
        xn_ref[...] = _rms(h_ref[...], gpre_ref[...]).astype(BF16)
        o_ref[...] = jnp.zeros_like(o_ref)

    gu = _dot(xn_ref[...], wgu_ref[...])
    g = gu[:, :tf]
    u = gu[:, tf:]
    a = (g * jax.nn.sigmoid(g) * u).astype(BF16)
    o_ref[...] += _dot(a, wd_ref[...])

    @pl.when(j == pl.num_programs(1) - 1)
    def _():
        o_ref[...] = h_ref[...] + _rms(o_ref[...], gpost_ref[...])


def _interleave_gate_up(wg, wu, tf):
    lead, (d, f) = wg.shape[:-2], wg.shape[-2:]
    parts = [w.reshape(lead + (d, f // tf, 1, tf)) for w in (wg, wu)]
    return jnp.concatenate(parts, axis=-2).reshape(lead + (d, 2 * f))


def _ffn(h, gpre, gpost, wgu, wd, sel, tm, tf):
    n, d = h.shape
    f = wd.shape[-2]
    vmem = (4 * tm * d * 4 + tm * d * 2 + 6 * d * tf * 2 + 6 * tm * tf * 4) / MIB + 4
    return pl.pallas_call(
        _ffn_kernel,
        grid=(n // tm, f // tf),
        in_specs=[pl.BlockSpec((tm, d), lambda i, j: (i, 0), pipeline_mode=pl.Buffered(1)),
                  pl.BlockSpec((1, d), lambda i, j: (0, 0)),
                  pl.BlockSpec((1, d), lambda i, j: (0, 0)),
                  pl.BlockSpec((None, None, d, 2 * tf), lambda i, j: sel + (0, j)),
                  pl.BlockSpec((None, None, tf, d), lambda i, j: sel + (j, 0))],
        out_specs=pl.BlockSpec((tm, d), lambda i, j: (i, 0)),
        out_shape=jax.ShapeDtypeStruct((n, d), F32),
        scratch_shapes=[pltpu.VMEM((tm, d), BF16)],
        compiler_params=_params(("parallel", "arbitrary"), vmem),
        name="ffn",
    )(h, gpre, 0.5 * gpost, wgu, wd)


def _inproj_kernel(h_ref, g_ref, w_ref, wgt_ref, z_ref, slab_ref, gt_ref):
    xn = _rms(h_ref[...], g_ref[...]).astype(BF16)
    z = _dot(xn, w_ref[...])
    z_ref[...] = z[:, :Z_WIDTH].astype(BF16)
    slab_ref[...] = z[:, Z_WIDTH:]
    gt_ref[0] = _dot_nt(wgt_ref[...], xn)


def _inproj(h, gain, w, wgt, tm, bsz):
    n, d = h.shape
    nts = n // bsz // tm
    wtot = Z_WIDTH + SLAB_WIDTH
    vmem = (2 * tm * d * 4 + d * wtot * 2 + 2 * tm * wtot * 4 + tm * wtot * 4) / MIB + 6
    return pl.pallas_call(
        _inproj_kernel,
        grid=(n // tm,),
        in_specs=[pl.BlockSpec((tm, d), lambda i: (i, 0)),
                  pl.BlockSpec((1, d), lambda i: (0, 0)),
                  pl.BlockSpec((d, wtot), lambda i: (0, 0), pipeline_mode=pl.Buffered(1)),
                  pl.BlockSpec((8, d), lambda i: (0, 0))],
        out_specs=[pl.BlockSpec((tm, Z_WIDTH), lambda i: (i, 0)),
                   pl.BlockSpec((tm, SLAB_WIDTH), lambda i: (i, 0)),
                   pl.BlockSpec((1, 8, tm), lambda i: (i // nts, 0, i % nts))],
        out_shape=[jax.ShapeDtypeStruct((n, Z_WIDTH), BF16),
                   jax.ShapeDtypeStruct((n, SLAB_WIDTH), F32),
                   jax.ShapeDtypeStruct((bsz, 8, n // bsz), F32)],
        compiler_params=_params(("parallel",), vmem),
        name="inproj",
    )(h, gain, w, wgt)


SCAN_COLS = 512
S5_HALVES = 2


def _s5_kernel(u_ref, wb_ref, tab_ref, wc_ref, d_ref, wglu_ref, bglu_ref, gain_ref,
               o_ref, x_ref, carry_ref):
    t = u_ref.shape[0]
    hl = SSM_LANES // S5_HALVES
    hw = SSM_WIDTH // S5_HALVES

    @pl.when(pl.program_id(1) == 0)
    def _():
        carry_ref[...] = jnp.zeros_like(carry_ref)

    u = u_ref[...]
    for k in range(S5_HALVES):
        x_ref[:, 2 * hl * k:2 * hl * (k + 1)] = _dot(u[:, hw * k:hw * (k + 1)], wb_ref[k])

    for c0 in range(0, SSM_LANES, SCAN_COLS):
        k, off = divmod(c0, hl)
        re = slice(2 * hl * k + off, 2 * hl * k + off + SCAN_COLS)
        im = slice(2 * hl * k + hl + off, 2 * hl * k + hl + off + SCAN_COLS)
        tabs = [(tab_ref[2 * j, :, c0:c0 + SCAN_COLS], tab_ref[2 * j + 1, :, c0:c0 + SCAN_COLS]) for j in range(4)]

        def block(i, carry, re=re, im=im, tabs=tabs):
            cr, ci = carry
            r0 = pl.multiple_of(i * V7X_SUBLANES, V7X_SUBLANES)
            xr = x_ref[pl.ds(r0, V7X_SUBLANES), re]
            xi = x_ref[pl.ds(r0, V7X_SUBLANES), im]
            for j, shift in enumerate((1, 2, 4)):
                ar, ai = tabs[j]
                rr = pltpu.roll(xr, shift, 0)
                ri = pltpu.roll(xi, shift, 0)
                xr, xi = xr + ar * rr - ai * ri, xi + ar * ri + ai * rr
            pr, pi = tabs[3]
            xr, xi = xr + pr * cr - pi * ci, xi + pr * ci + pi * cr
            x_ref[pl.ds(r0, V7X_SUBLANES), re] = xr
            x_ref[pl.ds(r0, V7X_SUBLANES), im] = xi
            last = V7X_SUBLANES - 1
            return (jnp.broadcast_to(xr[last:, :], xr.shape), jnp.broadcast_to(xi[last:, :], xi.shape))

        cr1, ci1 = lax.fori_loop(0, t // V7X_SUBLANES, block, (carry_ref[:, re], carry_ref[:, im]))
        carry_ref[:, re] = cr1
        carry_ref[:, im] = ci1

    y = jnp.concatenate([_dot(x_ref[:, 2 * hl * k:2 * hl * (k + 1)].astype(BF16), wc_ref[k])
                         for k in range(S5_HALVES)], axis=-1)
    y = y + d_ref[...] * u.astype(F32)
    g = jax.nn.gelu(y)
    out = g * jax.nn.sigmoid(_dot(g.astype(BF16), wglu_ref[...]) + bglu_ref[...])
    o_ref[...] = _rms(out, gain_ref[...]).astype(o_ref.dtype)


def _s5(z, wb, tabs, wc, d_skip, wglu, bglu, gain, bsz, seq, t):
    n = bsz * seq
    nt = seq // t
    nl2 = 2 * SSM_LANES
    const = lambda b, i: (0, 0)
    vmem = (t * nl2 * 4 * 3 + 2 * SSM_WIDTH * nl2 * 2 * 2 + 8 * 8 * SSM_LANES * 4 * 2) / MIB + 8
    return pl.pallas_call(
        _s5_kernel,
        grid=(bsz, nt),
        in_specs=[pl.BlockSpec((t, SSM_WIDTH), lambda b, i: (b * nt + i, Z_SU // SSM_WIDTH)),
                  pl.BlockSpec((S5_HALVES, SSM_WIDTH // S5_HALVES, nl2 // S5_HALVES), lambda b, i: (0, 0, 0)),
                  pl.BlockSpec((8, V7X_SUBLANES, SSM_LANES), lambda b, i: (0, 0, 0)),
                  pl.BlockSpec((S5_HALVES, nl2 // S5_HALVES, SSM_WIDTH // S5_HALVES), lambda b, i: (0, 0, 0)),
                  pl.BlockSpec((1, SSM_WIDTH), const),
                  pl.BlockSpec((SSM_WIDTH, SSM_WIDTH), const),
                  pl.BlockSpec((1, SSM_WIDTH), const),
                  pl.BlockSpec((1, SSM_WIDTH), const)],
        out_specs=pl.BlockSpec((t, SSM_WIDTH), lambda b, i: (b * nt + i, 0)),
        out_shape=jax.ShapeDtypeStruct((n, SSM_WIDTH), BF16),
        scratch_shapes=[pltpu.VMEM((t, nl2), F32), pltpu.VMEM((V7X_SUBLANES, nl2), F32)],
        compiler_params=_params(("arbitrary", "arbitrary"), vmem),
        name="s5",
    )(z, wb, tabs, wc, d_skip, wglu, bglu, gain)


def _s5_tables(a_re, a_im, b_re, b_im, c_re, c_im, log_dt):
    lr, li = a_re.astype(F32), a_im.astype(F32)
    dt = jnp.exp(log_dt.astype(F32))[:, None]
    mag = jnp.exp(lr * dt)
    ar, ai = mag * jnp.cos(li * dt), mag * jnp.sin(li * dt)
    den = lr * lr + li * li
    nr, ni = ar - 1.0, ai
    zr = (nr * lr + ni * li) / den
    zi = (ni * lr - nr * li) / den
    br_, bi_ = b_re.astype(F32), b_im.astype(F32)
    bbr = zr[..., None] * br_ - zi[..., None] * bi_
    bbi = zr[..., None] * bi_ + zi[..., None] * br_
    eye = jnp.eye(SSM_GROUPS, dtype=F32)
    wb_re = jnp.einsum('gnc,gh->gchn', bbr, eye).reshape(SSM_WIDTH, SSM_LANES)
    wb_im = jnp.einsum('gnc,gh->gchn', bbi, eye).reshape(SSM_WIDTH, SSM_LANES)
    hw, hl = SSM_WIDTH // S5_HALVES, SSM_LANES // S5_HALVES
    wb = jnp.stack([jnp.concatenate([w[hw * k:hw * (k + 1), hl * k:hl * (k + 1)] for w in (wb_re, wb_im)], axis=1)
                    for k in range(S5_HALVES)]).astype(BF16)
    wc_re = jnp.einsum('gcn,gh->gnhc', c_re.astype(F32), eye).reshape(SSM_LANES, SSM_WIDTH)
    wc_im = jnp.einsum('gcn,gh->gnhc', c_im.astype(F32), eye).reshape(SSM_LANES, SSM_WIDTH)
    wc = jnp.stack([jnp.concatenate([w[hl * k:hl * (k + 1), hw * k:hw * (k + 1)] for w in (wc_re, -wc_im)], axis=0)
                    for k in range(S5_HALVES)]).astype(BF16)

    a1 = (ar.reshape(1, SSM_LANES), ai.reshape(1, SSM_LANES))

    def cmul(p, q):
        return (p[0] * q[0] - p[1] * q[1], p[0] * q[1] + p[1] * q[0])

    pows = [a1]
    for _ in range(7):
        pows.append(cmul(pows[-1], a1))
    row = jnp.arange(V7X_SUBLANES)[:, None]
    tabs = []
    for shift in (1, 2, 4):
        pr, pi = pows[shift - 1]
        keep = row >= shift
        tabs += [jnp.where(keep, pr, 0.0), jnp.where(keep, pi, 0.0)]
    tabs += [jnp.concatenate([p[0] for p in pows], axis=0), jnp.concatenate([p[1] for p in pows], axis=0)]
    return wb, jnp.stack(tabs).astype(F32), wc


def _ret_kernel(q_ref, k_ref, v_ref, g_ref, rope_ref, dmask_ref, qdec_ref, kdec_ref, cdec_ref, gain_ref,
                o_ref, state_ref):
    @pl.when(pl.program_id(0) == 0)
    def _():
        state_ref[...] = jnp.zeros_like(state_ref)

    for b in range(q_ref.shape[0]):
        _ret_batch(q_ref.at[b], k_ref.at[b], v_ref.at[b], g_ref.at[b], rope_ref.at[b], dmask_ref, qdec_ref, kdec_ref,
                   cdec_ref, gain_ref, o_ref.at[b], state_ref.at[b])


def _ret_batch(q_ref, k_ref, v_ref, g_ref, rope_ref, dmask_ref, qdec_ref, kdec_ref, cdec_ref, gain_ref,
               o_ref, state_ref):
    t = q_ref.shape[0]
    for c in range(t // CHUNK):
        rows = slice(c * CHUNK, (c + 1) * CHUNK)
        rope = rope_ref[rows, :]
        q = _apply_rope(q_ref[rows, :].astype(F32), rope, HEADS)
        k = _apply_rope(k_ref[rows, :].astype(F32), rope, HEADS) * (QK ** -0.5)
        for h in range(HEADS):
            qh = q[:, h * QK:(h + 1) * QK].astype(BF16)
            kh = k[:, h * QK:(h + 1) * QK]
            vh = v_ref[rows, h * HV:(h + 1) * HV]
            s = _dot_nt(qh, kh.astype(BF16)) * dmask_ref[h]
            inner = _dot(s.astype(BF16), vh)
            r_old = state_ref[h]
            cross = _dot(qh, r_old.astype(BF16)) * qdec_ref[h]
            state_ref[h] = cdec_ref[h] * r_old + _dot_tn((kh * kdec_ref[h]).astype(BF16), vh)
            y = _rms(inner + cross, gain_ref[:, h * HV:(h + 1) * HV])
            gate = g_ref[rows, h * HV:(h + 1) * HV].astype(F32)
            o_ref[rows, h * HV:(h + 1) * HV] = (gate * jax.nn.sigmoid(gate) * y).astype(o_ref.dtype)


def _ret_tables():
    log_g = jnp.log1p(-jnp.exp2(-5.0 - jnp.arange(HEADS, dtype=F32)))
    idx = jnp.arange(CHUNK)
    diff = idx[:, None] - idx[None, :]
    causal = diff >= 0
    dmask = jnp.where(causal, jnp.exp(jnp.where(causal, diff, 0).astype(F32) * log_g[:, None, None]), 0.0)
    q_decay = jnp.exp((idx + 1).astype(F32)[None, :] * log_g[:, None])
    k_decay = jnp.exp((CHUNK - 1 - idx).astype(F32)[None, :] * log_g[:, None])
    chunk_decay = jnp.exp(CHUNK * log_g)
    qdec = jnp.broadcast_to(q_decay[:, :, None], (HEADS, CHUNK, HV))
    kdec = jnp.broadcast_to(k_decay[:, :, None], (HEADS, CHUNK, QK))
    cdec = jnp.broadcast_to(chunk_decay[:, None, None], (HEADS, QK, HV))
    return dmask, qdec, kdec, cdec


def _retention(z, rope, gain, bsz, seq, t):
    dmask, qdec, kdec, cdec = _ret_tables()
    c3 = lambda i: (0, 0, 0)

    def col(width, lane0):
        return pl.BlockSpec((bsz, t, width), lambda i: (0, i, lane0 // width))

    return pl.pallas_call(
        _ret_kernel,
        grid=(seq // t,),
        in_specs=[col(256, Z_RQ), col(256, Z_RK), col(512, Z_RV), col(512, Z_RG),
                  pl.BlockSpec((bsz, t, 128), lambda i: (0, i, 0)),
                  pl.BlockSpec((HEADS, CHUNK, CHUNK), c3),
                  pl.BlockSpec((HEADS, CHUNK, HV), c3),
                  pl.BlockSpec((HEADS, CHUNK, QK), c3),
                  pl.BlockSpec((HEADS, QK, HV), c3),
                  pl.BlockSpec((1, HEADS * HV), lambda i: (0, 0))],
        out_specs=pl.BlockSpec((bsz, t, HEADS * HV), lambda i: (0, i, 0)),
        out_shape=jax.ShapeDtypeStruct((bsz, seq, HEADS * HV), BF16),
        scratch_shapes=[pltpu.VMEM((bsz, HEADS, QK, HV), F32)],
        compiler_params=_params(("arbitrary",), 40),
        name="retention",
    )(z, z, z, z, rope, dmask, qdec, kdec, cdec, gain)


TAIL = V7X_SUBLANES


def _mlstm_kernel(q_ref, k_ref, v_ref, og_ref, slab_ref, gt_ref, cw_ref, cb_ref, brow_ref, bcol_ref,
                  tril_ref, triu_ref, gain_ref, o_ref, xbuf_ref, c_ref, m_ref):
    @pl.when(pl.program_id(0) == 0)
    def _():
        xbuf_ref[:, 0:TAIL, :] = jnp.zeros((xbuf_ref.shape[0], TAIL, xbuf_ref.shape[2]), F32)
        c_ref[...] = jnp.zeros_like(c_ref)
        m_ref[...] = jnp.zeros_like(m_ref)

    for b in range(q_ref.shape[0]):
        _mlstm_batch(q_ref.at[b], k_ref.at[b], v_ref.at[b], og_ref.at[b], slab_ref.at[b], gt_ref.at[b], cw_ref, cb_ref,
                     brow_ref, bcol_ref, tril_ref, triu_ref, gain_ref, o_ref.at[b], xbuf_ref.at[b], c_ref.at[b],
                     m_ref.at[b])


def _mlstm_batch(q_ref, k_ref, v_ref, og_ref, slab_ref, gt_ref, cw_ref, cb_ref, brow_ref, bcol_ref,
                 tril_ref, triu_ref, gain_ref, o_ref, xbuf_ref, c_ref, m_ref):
    t = q_ref.shape[0]

    xbuf_ref[TAIL:, 0:HEADS * QK] = q_ref[...].astype(F32)
    xbuf_ref[TAIL:, HEADS * QK:] = k_ref[...].astype(F32)
    acc = cb_ref[...] + cw_ref[MLSTM_CONV - 1:MLSTM_CONV, :] * xbuf_ref[TAIL:, :]
    for j in range(MLSTM_CONV - 1):
        off = TAIL - (MLSTM_CONV - 1) + j
        acc = acc + cw_ref[j:j + 1, :] * xbuf_ref[off:off + t, :]
    xbuf_ref[0:TAIL, :] = xbuf_ref[t:t + TAIL, :]
    qk = acc * jax.nn.sigmoid(acc)

    gr = gt_ref[...] + brow_ref[...]
    is_f_row = lax.broadcasted_iota(jnp.int32, gr.shape, 0) >= HEADS
    lf_r = jnp.where(is_f_row, jax.nn.log_sigmoid(gr), 0.0)
    gc = slab_ref[...] + bcol_ref[...]
    lane = lax.broadcasted_iota(jnp.int32, gc.shape, 1)
    is_f_col = (lane >= SLAB_F) & (lane < SLAB_F + HEADS)
    lf_c = jnp.where(is_f_col, jax.nn.log_sigmoid(gc), 0.0)

    row = lax.broadcasted_iota(jnp.int32, (CHUNK, CHUNK), 0)
    colid = lax.broadcasted_iota(jnp.int32, (CHUNK, CHUNK), 1)
    causal = colid <= row
    ones_col = (lax.broadcasted_iota(jnp.int32, (CHUNK, HV), 1) == 0).astype(BF16)

    for c in range(t // CHUNK):
        rows = slice(c * CHUNK, (c + 1) * CHUNK)
        bh_r = _split_dot(triu_ref[...], lf_r[:, rows], nt=True)
        bh_c = _split_dot(tril_ref[...], lf_c[rows, :], nt=False)
        a_r = gr[0:HEADS, rows] - bh_r[HEADS:, :]
        a_c = gc[rows, SLAB_I:SLAB_I + HEADS] - bh_c[:, SLAB_F:SLAB_F + HEADS]
        bh_c = bh_c[:, SLAB_F:SLAB_F + HEADS]
        for h in range(HEADS):
            qh = (qk[rows, h * QK:(h + 1) * QK] * (QK ** -0.5)).astype(BF16)
            kh = qk[rows, (HEADS + h) * QK:(HEADS + h + 1) * QK]
            vaug = jnp.concatenate([v_ref[rows, h * HV:(h + 1) * HV], ones_col], axis=-1)
            m_old = m_ref[h][0:1, 0:1]
            amat = jnp.where(causal, a_r[h:h + 1, :], -jnp.inf)
            big_m = jnp.maximum(jnp.max(amat, axis=-1, keepdims=True), m_old)
            dmat = jnp.exp(amat - big_m)
            s = _dot_nt(qh, kh.astype(BF16)) * dmat
            sc = jnp.exp(m_old - big_m)
            c_old = c_ref[h]
            tot = _dot(s.astype(BF16), vaug) + sc * _dot(qh, c_old.astype(BF16))
            num = tot[:, :HV]
            den = tot[:, HV:HV + 1]
            m_row = bh_c[:, h:h + 1] + big_m
            den = jnp.maximum(jnp.abs(den), jnp.exp(-m_row))
            hout = num / den
            m_last = big_m[CHUNK - 1:CHUNK, :]
            b_last = bh_c[CHUNK - 1:CHUNK, h:h + 1]
            wk = jnp.exp(a_c[:, h:h + 1] - m_last)
            dec = jnp.exp(m_old - m_last)
            c_ref[h] = dec * c_old + _dot_tn((kh * wk).astype(BF16), vaug)
            m_ref[h] = jnp.broadcast_to(b_last + m_last, m_ref.shape[1:])
            y = _rms(hout, gain_ref[:, h * HV:(h + 1) * HV])
            og = og_ref[rows, h * HV:(h + 1) * HV].astype(F32)
            o_ref[rows, h * HV:(h + 1) * HV] = (jax.nn.sigmoid(og) * y).astype(o_ref.dtype)


def _mlstm(z, slab, gt, conv_w, conv_b, b_i, b_f, gain, bsz, seq, t):
    idx = jnp.arange(CHUNK)
    tril = (idx[None, :] <= idx[:, None]).astype(BF16)
    triu = tril.T
    brow = jnp.concatenate([b_i, b_f]).astype(F32).reshape(2 * HEADS, 1)
    bcol = jnp.zeros((1, SLAB_WIDTH), F32).at[0, SLAB_I:SLAB_I + 2 * HEADS].set(brow[:, 0])
    c2 = lambda i: (0, 0)

    def col(width, lane0):
        return pl.BlockSpec((bsz, t, width), lambda i: (0, i, lane0 // width))

    return pl.pallas_call(
        _mlstm_kernel,
        grid=(seq // t,),
        in_specs=[col(256, Z_MQ), col(256, Z_MK), col(512, Z_MV), col(512, Z_MO),
                  pl.BlockSpec((bsz, t, SLAB_WIDTH), lambda i: (0, i, 0)),
                  pl.BlockSpec((bsz, 2 * HEADS, t), lambda i: (0, 0, i)),
                  pl.BlockSpec((MLSTM_CONV, 2 * HEADS * QK), c2),
                  pl.BlockSpec((1, 2 * HEADS * QK), c2),
                  pl.BlockSpec((2 * HEADS, 1), c2),
                  pl.BlockSpec((1, SLAB_WIDTH), c2),
                  pl.BlockSpec((CHUNK, CHUNK), c2),
                  pl.BlockSpec((CHUNK, CHUNK), c2),
                  pl.BlockSpec((1, HEADS * HV), c2)],
        out_specs=pl.BlockSpec((bsz, t, HEADS * HV), lambda i: (0, i, 0)),
        out_shape=jax.ShapeDtypeStruct((bsz, seq, HEADS * HV), BF16),
        scratch_shapes=[pltpu.VMEM((bsz, t + TAIL, 2 * HEADS * QK), F32),
                        pltpu.VMEM((bsz, HEADS, QK, 2 * HV), F32),
                        pltpu.VMEM((bsz, HEADS, V7X_SUBLANES, V7X_LANES), F32)],
        compiler_params=_params(("arbitrary",), 48),
        name="mlstm",
    )(z, z, z, z, slab, gt, conv_w.astype(F32), conv_b.astype(F32).reshape(1, -1), brow, bcol, tril, triu, gain)


VT_ROWS = HV + 16


def _mla_proj_kernel(cq_ref, ckv_ref, slab_ref, rope_ref, qn_ref, kvn_ref, wuq_ref, wuk_ref, wuvt_ref,
                     q_ref, k_ref, vt_ref):
    rope = rope_ref[...]
    cq = _rms(cq_ref[...].astype(F32), qn_ref[...]).astype(BF16)
    ckv = _rms(ckv_ref[...].astype(F32), kvn_ref[...]).astype(BF16)
    q = _dot(cq, wuq_ref[...])
    kn = _dot(ckv, wuk_ref[...])
    vt = _dot_nt(wuvt_ref[...], ckv)
    scale = (MLA_QK ** -0.5) * math.log2(math.e)
    q_rope = _apply_rope(q[:, HEADS * MLA_NOPE:], rope, HEADS)
    k_rope = _apply_rope(slab_ref[:, 0:QK], rope, 1)
    pad = VT_ROWS - HV
    ones_row = (lax.broadcasted_iota(jnp.int32, (pad, vt.shape[1]), 0) == 0).astype(BF16)
    for h in range(HEADS):
        q_ref[0, h, :, 0:MLA_NOPE] = (q[:, h * MLA_NOPE:(h + 1) * MLA_NOPE] * scale).astype(BF16)
        q_ref[0, h, :, MLA_NOPE:] = (q_rope[:, h * QK:(h + 1) * QK] * scale).astype(BF16)
        k_ref[0, h, :, 0:MLA_NOPE] = kn[:, h * MLA_NOPE:(h + 1) * MLA_NOPE].astype(BF16)
        k_ref[0, h, :, MLA_NOPE:] = k_rope.astype(BF16)
        vt_ref[0, h, 0:HV, :] = vt[h * HV:(h + 1) * HV, :].astype(BF16)
        vt_ref[0, h, HV:, :] = ones_row


def _mla_proj(z, slab, rope, q_norm, kv_norm, wuq, wuk, wuvt, bsz, seq, t):
    nt = seq // t
    c2 = lambda b, i: (0, 0)
    return pl.pallas_call(
        _mla_proj_kernel,
        grid=(bsz, nt),
        in_specs=[pl.BlockSpec((t, MLA_Q_RANK), lambda b, i: (b * nt + i, Z_CQ // MLA_Q_RANK)),
                  pl.BlockSpec((t, MLA_KV_RANK), lambda b, i: (b * nt + i, Z_CKV // MLA_KV_RANK)),
                  pl.BlockSpec((t, SLAB_WIDTH), lambda b, i: (b * nt + i, 0)),
                  pl.BlockSpec((t, 128), lambda b, i: (b * nt + i, 0)),
                  pl.BlockSpec((1, MLA_Q_RANK), c2),
                  pl.BlockSpec((1, MLA_KV_RANK), c2),
                  pl.BlockSpec((MLA_Q_RANK, HEADS * MLA_QK), c2),
                  pl.BlockSpec((MLA_KV_RANK, HEADS * MLA_NOPE), c2),
                  pl.BlockSpec((HEADS * HV, MLA_KV_RANK), c2)],
        out_specs=[pl.BlockSpec((1, HEADS, t, MLA_QK), lambda b, i: (b, 0, i, 0)),
                   pl.BlockSpec((1, HEADS, t, MLA_QK), lambda b, i: (b, 0, i, 0)),
                   pl.BlockSpec((1, HEADS, VT_ROWS, t), lambda b, i: (b, 0, 0, i))],
        out_shape=[jax.ShapeDtypeStruct((bsz, HEADS, seq, MLA_QK), BF16),
                   jax.ShapeDtypeStruct((bsz, HEADS, seq, MLA_QK), BF16),
                   jax.ShapeDtypeStruct((bsz, HEADS, VT_ROWS, seq), BF16)],
        compiler_params=_params(("parallel", "parallel"), 40),
        name="mla_proj",
    )(z, z, slab, rope, q_norm, kv_norm, wuq, wuk, wuvt)


FLASH_SUB = 1024


def _flash_kernel(q_ref, k_ref, vt_ref, gain_ref, o_ref, m_ref, acc_ref):
    qi = pl.program_id(1)
    ki = pl.program_id(2)
    tk = k_ref.shape[2]
    sub = min(FLASH_SUB, tk)

    @pl.when(ki == 0)
    def _():
        m_ref[...] = jnp.full_like(m_ref, -jnp.inf)
        acc_ref[...] = jnp.zeros_like(acc_ref)

    def step(diagonal):
        for h in range(HEADS):
            for c in range(tk // sub):
                c0 = c * sub if diagonal else 0
                keys = slice(c * sub, (c + 1) * sub)
                st = _dot_nt(k_ref[0, h, keys, :], q_ref[0, h, c0:, :])
                if diagonal:
                    key = c * sub + lax.broadcasted_iota(jnp.int32, st.shape, 0)
                    qry = c0 + lax.broadcasted_iota(jnp.int32, st.shape, 1)
                    st = jnp.where(key <= qry, st, -jnp.inf)
                m_old = m_ref[h, 0:1, c0:]
                m_new = jnp.maximum(m_old, jnp.max(st, axis=0, keepdims=True))
                alpha = jnp.exp2(m_old - m_new)
                pt = jnp.exp2(st - m_new).astype(BF16)
                acc_ref[h, :, c0:] = alpha * acc_ref[h, :, c0:] + _dot(vt_ref[0, h, :, keys], pt)
                m_ref[h, :, c0:] = jnp.broadcast_to(m_new, (V7X_SUBLANES, m_new.shape[1]))

    @pl.when(ki < qi)
    def _():
        step(False)

    @pl.when(ki == qi)
    def _():
        step(True)
        for h in range(HEADS):
            acc = acc_ref[h]
            yt = acc[0:HV, :] / acc[HV:HV + 1, :]
            yt = yt * lax.rsqrt(jnp.mean(yt * yt, axis=0, keepdims=True) + EPS)
            o_ref[:, h * HV:(h + 1) * HV] = (yt.T * gain_ref[:, h * HV:(h + 1) * HV]).astype(o_ref.dtype)


def _flash(q, k, vt, gain, t):
    bsz, _, seq, _ = q.shape
    nt = seq // t
    return pl.pallas_call(
        _flash_kernel,
        grid=(bsz, nt, nt),
        in_specs=[pl.BlockSpec((1, HEADS, t, MLA_QK), lambda b, i, j: (b, 0, i, 0)),
                  pl.BlockSpec((1, HEADS, t, MLA_QK), lambda b, i, j: (b, 0, jnp.minimum(i, j), 0)),
                  pl.BlockSpec((1, HEADS, VT_ROWS, t), lambda b, i, j: (b, 0, 0, jnp.minimum(i, j))),
                  pl.BlockSpec((1, HEADS * HV), lambda b, i, j: (0, 0))],
        out_specs=pl.BlockSpec((t, HEADS * HV), lambda b, i, j: (b * nt + i, 0)),
        out_shape=jax.ShapeDtypeStruct((bsz * seq, HEADS * HV), BF16),
        scratch_shapes=[pltpu.VMEM((HEADS, V7X_SUBLANES, t), F32),
                        pltpu.VMEM((HEADS, VT_ROWS, t), F32)],
        compiler_params=_params(("parallel", "parallel", "arbitrary"), 56),
        name="mla_flash",
    )(q, k, vt, gain)


def _outproj_kernel(h_ref, a_ref, b_ref, c_ref, d_ref, w_ref, g_ref, o_ref):
    y = _dot(a_ref[...], w_ref[0:512, :])
    y += _dot(b_ref[...], w_ref[512:1024, :])
    y += _dot(c_ref[...], w_ref[1024:1536, :])
    y += _dot(d_ref[...], w_ref[1536:2048, :])
    o_ref[...] = h_ref[...] + _rms(y, g_ref[...])


def _outproj(h, mixes, w, gain, tm):
    n, d = h.shape
    row = lambda i: (i, 0)
    vmem = (4 * tm * d * 4 + d * d * 2 + 8 * tm * 512 * 2 + 2 * tm * d * 4) / MIB + 6
    return pl.pallas_call(
        _outproj_kernel,
        grid=(n // tm,),
        in_specs=[pl.BlockSpec((tm, d), row)] + [pl.BlockSpec((tm, 512), row)] * 4 +
                 [pl.BlockSpec((d, d), lambda i: (0, 0), pipeline_mode=pl.Buffered(1)),
                  pl.BlockSpec((1, d), lambda i: (0, 0))],
        out_specs=pl.BlockSpec((tm, d), row),
        out_shape=jax.ShapeDtypeStruct((n, d), F32),
        compiler_params=_params(("parallel",), vmem),
        name="outproj",
    )(h, *mixes, w, gain)


def _ple_kernel(h_ref, p_ref, g6_ref, g7_ref, wg_ref, wp_ref, o_ref):
    h = h_ref[...]
    gate = jax.nn.sigmoid(_dot(_rms(h, g6_ref[...]).astype(BF16), wg_ref[...]))
    emb = _rms(_dot(p_ref[...].astype(BF16), wp_ref[...]), g7_ref[...])
    o_ref[...] = h + gate * emb


def _ple(h, p, layer, g6, g7, wg, wp, tm):
    n, d = h.shape
    row = lambda i: (i, 0)
    c2 = lambda i: (0, 0)
    vmem = (4 * tm * d * 4 + d * d * 2 + 2 * PLE_DIM * d * 2 + 3 * tm * d * 4) / MIB + 6
    return pl.pallas_call(
        _ple_kernel,
        grid=(n // tm,),
        in_specs=[pl.BlockSpec((tm, d), row), pl.BlockSpec((None, tm, PLE_DIM), lambda i: (layer, i, 0)),
                  pl.BlockSpec((1, d), c2), pl.BlockSpec((1, d), c2),
                  pl.BlockSpec((d, d), c2, pipeline_mode=pl.Buffered(1)),
                  pl.BlockSpec((PLE_DIM, d), c2)],
        out_specs=pl.BlockSpec((tm, d), row),
        out_shape=jax.ShapeDtypeStruct((n, d), F32),
        compiler_params=_params(("parallel",), vmem),
        name="ple",
    )(h, p, g6, g7, wg, wp)


def _permute_w_in(w_in):
    sizes = (SSM_WIDTH, 256, 256, 512, 512, 4, 4, MLA_Q_RANK, MLA_KV_RANK, QK, 256, 256, 512, 512)
    offs = np.concatenate([[0], np.cumsum(sizes)])
    (s_u, m_q, m_k, m_v, m_o, m_i, m_f, a_cq, a_ckv, a_kr, r_q, r_k, r_v, r_g) = [
        w_in[:, int(offs[i]):int(offs[i + 1])] for i in range(len(sizes))]
    pad = jnp.zeros((w_in.shape[0], SLAB_WIDTH - QK - 2 * HEADS), w_in.dtype)
    main = jnp.concatenate([s_u, m_v, m_o, r_v, r_g, m_q, m_k, a_ckv, r_q, r_k, a_cq,
                            a_kr, m_i, m_f, pad], axis=1).astype(BF16)
    gates_t = jnp.concatenate([m_i, m_f], axis=1).T.astype(BF16)
    return main, gates_t


def _permute_mla(w_uq, w_ukv):
    uq = w_uq.reshape(MLA_Q_RANK, HEADS, MLA_QK)
    uq = jnp.concatenate([uq[:, :, :MLA_NOPE].reshape(MLA_Q_RANK, -1),
                          uq[:, :, MLA_NOPE:].reshape(MLA_Q_RANK, -1)], axis=1)
    ukv = w_ukv.reshape(MLA_KV_RANK, HEADS, MLA_NOPE + HV)
    uk = ukv[:, :, :MLA_NOPE].reshape(MLA_KV_RANK, -1)
    uvt = ukv[:, :, MLA_NOPE:].reshape(MLA_KV_RANK, -1).T
    return uq.astype(BF16), uk.astype(BF16), uvt.astype(BF16)


def _tile(n, want):
    t = min(n, want)
    assert n % t == 0, (n, t)
    return t


def kernel(x, p, positions, norm_gains, ffn_w_gate, ffn_w_up, ffn_w_down, w_in, w_out, mix_norm_gain, ssm_a_re, ssm_a_im, ssm_b_re, ssm_b_im, ssm_c_re, ssm_c_im, ssm_d, ssm_log_dt, ssm_w_glu, ssm_b_glu, mlstm_conv_w, mlstm_conv_b, mlstm_b_i, mlstm_b_f, mla_q_norm, mla_kv_norm, mla_w_uq, mla_w_ukv, ple_w_proj, ple_w_gate):
    bsz, seq, d = x.shape
    n = bsz * seq
    depth = p.shape[0]
    assert d == D_MODEL and seq % CHUNK == 0

    tm_ffn = _tile(n, 1024)
    tf_ffn = _tile(D_FF, 512)
    tm_proj = _tile(n, 256)
    tm_row = _tile(n, 512)
    t_seq = _tile(seq, 512)
    t_s5 = _tile(seq, 512)
    t_att = _tile(seq, 2048)

    rope = _rope_table(positions, _tile(n, 2048))
    h = x.reshape(n, d).astype(F32)
    gains = norm_gains.astype(F32).reshape(depth, 8, 1, d)
    mixg = mix_norm_gain.astype(F32).reshape(depth, 4, 1, 512)
    wgu_all = _interleave_gate_up(ffn_w_gate.astype(BF16), ffn_w_up.astype(BF16), tf_ffn)
    wd_all = ffn_w_down.astype(BF16)
    p_rows = p.reshape(depth, n, PLE_DIM)

    for i in range(depth):
        g = gains[i]
        h = _ffn(h, g[0], g[1], wgu_all, wd_all, (i, 0), tm_ffn, tf_ffn)

        w_main, w_gt = _permute_w_in(w_in[i])
        z, slab, gt = _inproj(h, g[2], w_main, w_gt, tm_proj, bsz)
        z3, slab3, rope3 = (a.reshape(bsz, seq, a.shape[-1]) for a in (z, slab, rope))

        wb, tabs, wc = _s5_tables(ssm_a_re[i], ssm_a_im[i], ssm_b_re[i], ssm_b_im[i],
                                  ssm_c_re[i], ssm_c_im[i], ssm_log_dt[i])
        out_a = _s5(z, wb, tabs, wc, ssm_d[i].astype(F32).reshape(1, -1), ssm_w_glu[i].astype(BF16),
                    ssm_b_glu[i].astype(F32).reshape(1, -1), mixg[i, 0], bsz, seq, t_s5)
        out_b = _mlstm(z3, slab3, gt, mlstm_conv_w[i], mlstm_conv_b[i], mlstm_b_i[i], mlstm_b_f[i],
                       mixg[i, 1], bsz, seq, t_seq).reshape(n, -1)
        wuq, wuk, wuvt = _permute_mla(mla_w_uq[i], mla_w_ukv[i])
        q, k, vt = _mla_proj(z, slab, rope, mla_q_norm[i].astype(F32).reshape(1, -1),
                             mla_kv_norm[i].astype(F32).reshape(1, -1), wuq, wuk, wuvt, bsz, seq, t_seq)
        out_c = _flash(q, k, vt, mixg[i, 2], t_att)
        out_d = _retention(z3, rope3, mixg[i, 3], bsz, seq, t_seq).reshape(n, -1)

        h = _outproj(h, (out_a, out_b, out_c, out_d), w_out[i].astype(BF16), g[3], tm_row)
        h = _ffn(h, g[4], g[5], wgu_all, wd_all, (i, 1), tm_ffn, tf_ffn)
        h = _ple(h, p_rows, i, g[6], g[7], ple_w_gate[i].astype(BF16), ple_w_proj[i].astype(BF16), tm_row)
    return h.reshape(bsz, seq, d).astype(x.dtype)
```

```python
import functools
import math

import jax
import jax.numpy as jnp
import numpy as np
from jax import lax
from jax.experimental import pallas as pl
from jax.experimental.pallas import tpu as pltpu

F32 = jnp.float32
BF16 = jnp.bfloat16

EPS = 1e-6
ROPE_BASE = 10000.0

D_MODEL = 2048
PLE_DIM = 256
D_FF = 5632

SSM_WIDTH = 512
SSM_GROUP = 16
SSM_GROUPS = SSM_WIDTH // SSM_GROUP
SSM_STATE = 64
SSM_LANES = SSM_GROUPS * SSM_STATE

HEADS = 4
QK = 64
HV = 128
CHUNK = 256
MLSTM_CONV = 4
MLA_Q_RANK = 384
MLA_KV_RANK = 256
MLA_NOPE = 128
MLA_QK = MLA_NOPE + QK

V7X_LANES = 128
V7X_SUBLANES = 8
V7X_VMEM_BYTES = 64 * 1024 * 1024
MIB = 1024 * 1024

Z_SU, Z_MV, Z_MO, Z_RV, Z_RG = 0, 512, 1024, 1536, 2048
Z_MQ, Z_MK, Z_CKV, Z_RQ, Z_RK = 2560, 2816, 3072, 3328, 3584
Z_CQ = 3840
Z_WIDTH = 4224
SLAB_WIDTH = 128
SLAB_I, SLAB_F = 64, 68


def _params(sem, vmem_mib):
    return pltpu.CompilerParams(dimension_semantics=sem, vmem_limit_bytes=int(vmem_mib * MIB))


def _rms(x, gain):
    ms = jnp.mean(x * x, axis=-1, keepdims=True)
    return x * lax.rsqrt(ms + EPS) * gain


def _dot(a, b):
    return jnp.dot(a, b, preferred_element_type=F32)


def _dot_nt(a, b):
    return lax.dot_general(a, b, (((1,), (1,)), ((), ())), preferred_element_type=F32)


def _dot_tn(a, b):
    return lax.dot_general(a, b, (((0,), (0,)), ((), ())), preferred_element_type=F32)


def _split_dot(tri, x, nt):
    hi = x.astype(BF16)
    lo = (x - hi.astype(F32)).astype(BF16)
    if nt:
        return _dot(hi, tri) + _dot(lo, tri)
    return _dot(tri, hi) + _dot(tri, lo)


def _rope_kernel(pos_ref, inv_ref, o_ref):
    ang = pos_ref[...].astype(F32) * inv_ref[...]
    lane = lax.broadcasted_iota(jnp.int32, ang.shape, 1)
    sgn = jnp.where(lane < 96, -1.0, 1.0).astype(F32)
    o_ref[...] = jnp.where(lane < 64, jnp.cos(ang), sgn * jnp.sin(ang))


def _rope_table(positions, tm):
    n = positions.size
    inv = 1.0 / (ROPE_BASE ** (jnp.arange(0, QK, 2, dtype=F32) / QK))
    inv = jnp.tile(inv, 4).reshape(1, 128)
    return pl.pallas_call(
        _rope_kernel,
        grid=(n // tm,),
        in_specs=[pl.BlockSpec((tm, 1), lambda i: (i, 0)),
                  pl.BlockSpec((1, 128), lambda i: (0, 0))],
        out_specs=pl.BlockSpec((tm, 128), lambda i: (i, 0)),
        out_shape=jax.ShapeDtypeStruct((n, 128), F32),
        compiler_params=_params(("parallel",), 32),
        name="rope_table",
    )(positions.reshape(n, 1), inv)


def _apply_rope(x, rope, nheads):
    width = nheads * QK
    cos = rope[:, :QK]
    sin = rope[:, QK:]
    if nheads > 1:
        cos = jnp.concatenate([cos] * nheads, axis=-1)
        sin = jnp.concatenate([sin] * nheads, axis=-1)
    if width >= V7X_LANES:
        lane = lax.broadcasted_iota(jnp.int32, x.shape, 1)
        first = (lane % QK) < (QK // 2)
        rot = jnp.where(first, pltpu.roll(x, width - QK // 2, 1), pltpu.roll(x, QK // 2, 1))
    else:
        rot = jnp.concatenate([x[:, QK // 2:], x[:, :QK // 2]], axis=-1)
    return x * cos + rot * sin


def _ffn_kernel(h_ref, gpre_ref, gpost_ref, wg_ref, wu_ref, wd_ref, o_ref, xn_ref):
    j = pl.program_id(1)

    @pl.when(j == 0)
    def _():
        xn_ref[...] = _rms(h_ref[...], gpre_ref[...]).astype(BF16)
        o_ref[...] = jnp.zeros_like(o_ref)

    x = xn_ref[...]
    g = _dot(x, wg_ref[...])
    u = _dot(x, wu_ref[...])
    a = (g * jax.nn.sigmoid(g) * u).astype(BF16)
    o_ref[...] += _dot(a, wd_ref[...])

    @pl.when(j == pl.num_programs(1) - 1)
    def _():
        o_ref[...] = h_ref[...] + _rms(o_ref[...], gpost_ref[...])


def _ffn(h, gpre, gpost, wg, wu, wd, tm, tf):
    n, d = h.shape
    f = wg.shape[1]
    vmem = (4 * tm * d * 4 + tm * d * 2 + 6 * d * tf * 2 + 6 * tm * tf * 4) / MIB + 4
    return pl.pallas_call(
        _ffn_kernel,
        grid=(n // tm, f // tf),
        in_specs=[pl.BlockSpec((tm, d), lambda i, j: (i, 0), pipeline_mode=pl.Buffered(1)),
                  pl.BlockSpec((1, d), lambda i, j: (0, 0)),
                  pl.BlockSpec((1, d), lambda i, j: (0, 0)),
                  pl.BlockSpec((d, tf), lambda i, j: (0, j)),
                  pl.BlockSpec((d, tf), lambda i, j: (0, j)),
                  pl.BlockSpec((tf, d), lambda i, j: (j, 0))],
        out_specs=pl.BlockSpec((tm, d), lambda i, j: (i, 0)),
        out_shape=jax.ShapeDtypeStruct((n, d), F32),
        scratch_shapes=[pltpu.VMEM((tm, d), BF16)],
        compiler_params=_params(("parallel", "arbitrary"), vmem),
        name="ffn",
    )(h, gpre, 0.5 * gpost, wg, wu, wd)


def _inproj_kernel(h_ref, g_ref, w_ref, wgt_ref, z_ref, slab_ref, gt_ref):
    xn = _rms(h_ref[...], g_ref[...]).astype(BF16)
    z = _dot(xn, w_ref[...])
    z_ref[...] = z[:, :Z_WIDTH].astype(BF16)
    slab_ref[...] = z[:, Z_WIDTH:]
    gt_ref[...] = _dot_nt(wgt_ref[...], xn)


def _inproj(h, gain, w, wgt, tm):
    n, d = h.shape
    wtot = Z_WIDTH + SLAB_WIDTH
    vmem = (2 * tm * d * 4 + d * wtot * 2 + 2 * tm * wtot * 4 + tm * wtot * 4) / MIB + 6
    return pl.pallas_call(
        _inproj_kernel,
        grid=(n // tm,),
        in_specs=[pl.BlockSpec((tm, d), lambda i: (i, 0)),
                  pl.BlockSpec((1, d), lambda i: (0, 0)),
                  pl.BlockSpec((d, wtot), lambda i: (0, 0), pipeline_mode=pl.Buffered(1)),
                  pl.BlockSpec((8, d), lambda i: (0, 0))],
        out_specs=[pl.BlockSpec((tm, Z_WIDTH), lambda i: (i, 0)),
                   pl.BlockSpec((tm, SLAB_WIDTH), lambda i: (i, 0)),
                   pl.BlockSpec((8, tm), lambda i: (0, i))],
        out_shape=[jax.ShapeDtypeStruct((n, Z_WIDTH), BF16),
                   jax.ShapeDtypeStruct((n, SLAB_WIDTH), F32),
                   jax.ShapeDtypeStruct((8, n), F32)],
        compiler_params=_params(("parallel",), vmem),
        name="inproj",
    )(h, gain, w, wgt)


SCAN_COLS = 512
S5_HALVES = 2


def _s5_kernel(u_ref, wb_ref, tab_ref, wc_ref, d_ref, wglu_ref, bglu_ref, gain_ref,
               o_ref, x_ref, carry_ref):
    t = u_ref.shape[0]
    hl = SSM_LANES // S5_HALVES
    hw = SSM_WIDTH // S5_HALVES

    @pl.when(pl.program_id(1) == 0)
    def _():
        carry_ref[...] = jnp.zeros_like(carry_ref)

    u = u_ref[...]
    for k in range(S5_HALVES):
        x_ref[:, 2 * hl * k:2 * hl * (k + 1)] = _dot(u[:, hw * k:hw * (k + 1)], wb_ref[k])

    for c0 in range(0, SSM_LANES, SCAN_COLS):
        k, off = divmod(c0, hl)
        re = slice(2 * hl * k + off, 2 * hl * k + off + SCAN_COLS)
        im = slice(2 * hl * k + hl + off, 2 * hl * k + hl + off + SCAN_COLS)
        tabs = [(tab_ref[2 * j, :, c0:c0 + SCAN_COLS], tab_ref[2 * j + 1, :, c0:c0 + SCAN_COLS]) for j in range(4)]

        def block(i, carry, re=re, im=im, tabs=tabs):
            cr, ci = carry
            r0 = pl.multiple_of(i * V7X_SUBLANES, V7X_SUBLANES)
            xr = x_ref[pl.ds(r0, V7X_SUBLANES), re]
            xi = x_ref[pl.ds(r0, V7X_SUBLANES), im]
            for j, shift in enumerate((1, 2, 4)):
                ar, ai = tabs[j]
                rr = pltpu.roll(xr, shift, 0)
                ri = pltpu.roll(xi, shift, 0)
                xr, xi = xr + ar * rr - ai * ri, xi + ar * ri + ai * rr
            pr, pi = tabs[3]
            xr, xi = xr + pr * cr - pi * ci, xi + pr * ci + pi * cr
            x_ref[pl.ds(r0, V7X_SUBLANES), re] = xr
            x_ref[pl.ds(r0, V7X_SUBLANES), im] = xi
            last = V7X_SUBLANES - 1
            return (jnp.broadcast_to(xr[last:, :], xr.shape), jnp.broadcast_to(xi[last:, :], xi.shape))

        cr1, ci1 = lax.fori_loop(0, t // V7X_SUBLANES, block, (carry_ref[:, re], carry_ref[:, im]))
        carry_ref[:, re] = cr1
        carry_ref[:, im] = ci1

    y = jnp.concatenate([_dot(x_ref[:, 2 * hl * k:2 * hl * (k + 1)].astype(BF16), wc_ref[k])
                         for k in range(S5_HALVES)], axis=-1)
    y = y + d_ref[...] * u.astype(F32)
    g = jax.nn.gelu(y)
    out = g * jax.nn.sigmoid(_dot(g.astype(BF16), wglu_ref[...]) + bglu_ref[...])
    o_ref[...] = _rms(out, gain_ref[...]).astype(o_ref.dtype)


def _s5(z, wb, tabs, wc, d_skip, wglu, bglu, gain, bsz, seq, t):
    n = bsz * seq
    nt = seq // t
    nl2 = 2 * SSM_LANES
    const = lambda b, i: (0, 0)
    vmem = (t * nl2 * 4 * 3 + 2 * SSM_WIDTH * nl2 * 2 * 2 + 8 * 8 * SSM_LANES * 4 * 2) / MIB + 8
    return pl.pallas_call(
        _s5_kernel,
        grid=(bsz, nt),
        in_specs=[pl.BlockSpec((t, SSM_WIDTH), lambda b, i: (b * nt + i, Z_SU // SSM_WIDTH)),
                  pl.BlockSpec((S5_HALVES, SSM_WIDTH // S5_HALVES, nl2 // S5_HALVES), lambda b, i: (0, 0, 0)),
                  pl.BlockSpec((8, V7X_SUBLANES, SSM_LANES), lambda b, i: (0, 0, 0)),
                  pl.BlockSpec((S5_HALVES, nl2 // S5_HALVES, SSM_WIDTH // S5_HALVES), lambda b, i: (0, 0, 0)),
                  pl.BlockSpec((1, SSM_WIDTH), const),
                  pl.BlockSpec((SSM_WIDTH, SSM_WIDTH), const),
                  pl.BlockSpec((1, SSM_WIDTH), const),
                  pl.BlockSpec((1, SSM_WIDTH), const)],
        out_specs=pl.BlockSpec((t, SSM_WIDTH), lambda b, i: (b * nt + i, 0)),
        out_shape=jax.ShapeDtypeStruct((n, SSM_WIDTH), BF16),
        scratch_shapes=[pltpu.VMEM((t, nl2), F32), pltpu.VMEM((V7X_SUBLANES, nl2), F32)],
        compiler_params=_params(("arbitrary", "arbitrary"), vmem),
        name="s5",
    )(z, wb, tabs, wc, d_skip, wglu, bglu, gain)


def _s5_tables(a_re, a_im, b_re, b_im, c_re, c_im, log_dt):
    lr, li = a_re.astype(F32), a_im.astype(F32)
    dt = jnp.exp(log_dt.astype(F32))[:, None]
    mag = jnp.exp(lr * dt)
    ar, ai = mag * jnp.cos(li * dt), mag * jnp.sin(li * dt)
    den = lr * lr + li * li
    nr, ni = ar - 1.0, ai
    zr = (nr * lr + ni * li) / den
    zi = (ni * lr - nr * li) / den
    br_, bi_ = b_re.astype(F32), b_im.astype(F32)
    bbr = zr[..., None] * br_ - zi[..., None] * bi_
    bbi = zr[..., None] * bi_ + zi[..., None] * br_
    eye = jnp.eye(SSM_GROUPS, dtype=F32)
    wb_re = jnp.einsum('gnc,gh->gchn', bbr, eye).reshape(SSM_WIDTH, SSM_LANES)
    wb_im = jnp.einsum('gnc,gh->gchn', bbi, eye).reshape(SSM_WIDTH, SSM_LANES)
    hw, hl = SSM_WIDTH // S5_HALVES, SSM_LANES // S5_HALVES
    wb = jnp.stack([jnp.concatenate([w[hw * k:hw * (k + 1), hl * k:hl * (k + 1)] for w in (wb_re, wb_im)], axis=1)
                    for k in range(S5_HALVES)]).astype(BF16)
    wc_re = jnp.einsum('gcn,gh->gnhc', c_re.astype(F32), eye).reshape(SSM_LANES, SSM_WIDTH)
    wc_im = jnp.einsum('gcn,gh->gnhc', c_im.astype(F32), eye).reshape(SSM_LANES, SSM_WIDTH)
    wc = jnp.stack([jnp.concatenate([w[hl * k:hl * (k + 1), hw * k:hw * (k + 1)] for w in (wc_re, -wc_im)], axis=0)
                    for k in range(S5_HALVES)]).astype(BF16)

    a1 = (ar.reshape(1, SSM_LANES), ai.reshape(1, SSM_LANES))

    def cmul(p, q):
        return (p[0] * q[0] - p[1] * q[1], p[0] * q[1] + p[1] * q[0])

    pows = [a1]
    for _ in range(7):
        pows.append(cmul(pows[-1], a1))
    row = jnp.arange(V7X_SUBLANES)[:, None]
    tabs = []
    for shift in (1, 2, 4):
        pr, pi = pows[shift - 1]
        keep = row >= shift
        tabs += [jnp.where(keep, pr, 0.0), jnp.where(keep, pi, 0.0)]
    tabs += [jnp.concatenate([p[0] for p in pows], axis=0), jnp.concatenate([p[1] for p in pows], axis=0)]
    return wb, jnp.stack(tabs).astype(F32), wc


def _ret_kernel(q_ref, k_ref, v_ref, g_ref, rope_ref, dmask_ref, qdec_ref, kdec_ref, cdec_ref, gain_ref,
                o_ref, state_ref):
    t = q_ref.shape[0]

    @pl.when(pl.program_id(1) == 0)
    def _():
        state_ref[...] = jnp.zeros_like(state_ref)

    for c in range(t // CHUNK):
        rows = slice(c * CHUNK, (c + 1) * CHUNK)
        rope = rope_ref[rows, :]
        q = _apply_rope(q_ref[rows, :].astype(F32), rope, HEADS)
        k = _apply_rope(k_ref[rows, :].astype(F32), rope, HEADS) * (QK ** -0.5)
        for h in range(HEADS):
            qh = q[:, h * QK:(h + 1) * QK].astype(BF16)
            kh = k[:, h * QK:(h + 1) * QK]
            vh = v_ref[rows, h * HV:(h + 1) * HV]
            s = _dot_nt(qh, kh.astype(BF16)) * dmask_ref[h]
            inner = _dot(s.astype(BF16), vh)
            r_old = state_ref[h]
            cross = _dot(qh, r_old.astype(BF16)) * qdec_ref[h]
            state_ref[h] = cdec_ref[h] * r_old + _dot_tn((kh * kdec_ref[h]).astype(BF16), vh)
            y = _rms(inner + cross, gain_ref[:, h * HV:(h + 1) * HV])
            gate = g_ref[rows, h * HV:(h + 1) * HV].astype(F32)
            o_ref[rows, h * HV:(h + 1) * HV] = (gate * jax.nn.sigmoid(gate) * y).astype(o_ref.dtype)


def _ret_tables():
    log_g = jnp.log1p(-jnp.exp2(-5.0 - jnp.arange(HEADS, dtype=F32)))
    idx = jnp.arange(CHUNK)
    diff = idx[:, None] - idx[None, :]
    causal = diff >= 0
    dmask = jnp.where(causal, jnp.exp(jnp.where(causal, diff, 0).astype(F32) * log_g[:, None, None]), 0.0)
    q_decay = jnp.exp((idx + 1).astype(F32)[None, :] * log_g[:, None])
    k_decay = jnp.exp((CHUNK - 1 - idx).astype(F32)[None, :] * log_g[:, None])
    chunk_decay = jnp.exp(CHUNK * log_g)
    qdec = jnp.broadcast_to(q_decay[:, :, None], (HEADS, CHUNK, HV))
    kdec = jnp.broadcast_to(k_decay[:, :, None], (HEADS, CHUNK, QK))
    cdec = jnp.broadcast_to(chunk_decay[:, None, None], (HEADS, QK, HV))
    return dmask, qdec, kdec, cdec


def _retention(z, rope, gain, bsz, seq, t):
    n = bsz * seq
    nt = seq // t
    dmask, qdec, kdec, cdec = _ret_tables()
    c3 = lambda b, i: (0, 0, 0)

    def col(width, lane0):
        return pl.BlockSpec((t, width), lambda b, i: (b * nt + i, lane0 // width))

    return pl.pallas_call(
        _ret_kernel,
        grid=(bsz, nt),
        in_specs=[col(256, Z_RQ), col(256, Z_RK), col(512, Z_RV), col(512, Z_RG),
                  pl.BlockSpec((t, 128), lambda b, i: (b * nt + i, 0)),
                  pl.BlockSpec((HEADS, CHUNK, CHUNK), c3),
                  pl.BlockSpec((HEADS, CHUNK, HV), c3),
                  pl.BlockSpec((HEADS, CHUNK, QK), c3),
                  pl.BlockSpec((HEADS, QK, HV), c3),
                  pl.BlockSpec((1, HEADS * HV), lambda b, i: (0, 0))],
        out_specs=pl.BlockSpec((t, HEADS * HV), lambda b, i: (b * nt + i, 0)),
        out_shape=jax.ShapeDtypeStruct((n, HEADS * HV), BF16),
        scratch_shapes=[pltpu.VMEM((HEADS, QK, HV), F32)],
        compiler_params=_params(("arbitrary", "arbitrary"), 32),
        name="retention",
    )(z, z, z, z, rope, dmask, qdec, kdec, cdec, gain)


TAIL = V7X_SUBLANES


def _mlstm_kernel(q_ref, k_ref, v_ref, og_ref, slab_ref, gt_ref, cw_ref, cb_ref, brow_ref, bcol_ref,
                  tril_ref, triu_ref, gain_ref, o_ref, xbuf_ref, c_ref, m_ref):
    t = q_ref.shape[0]
    width = 2 * HEADS * QK

    @pl.when(pl.program_id(1) == 0)
    def _():
        xbuf_ref[0:TAIL, :] = jnp.zeros((TAIL, width), F32)
        c_ref[...] = jnp.zeros_like(c_ref)
        m_ref[...] = jnp.zeros_like(m_ref)

    xbuf_ref[TAIL:, 0:HEADS * QK] = q_ref[...].astype(F32)
    xbuf_ref[TAIL:, HEADS * QK:] = k_ref[...].astype(F32)
    acc = cb_ref[...] + cw_ref[MLSTM_CONV - 1:MLSTM_CONV, :] * xbuf_ref[TAIL:, :]
    for j in range(MLSTM_CONV - 1):
        off = TAIL - (MLSTM_CONV - 1) + j
        acc = acc + cw_ref[j:j + 1, :] * xbuf_ref[off:off + t, :]
    xbuf_ref[0:TAIL, :] = xbuf_ref[t:t + TAIL, :]
    qk = acc * jax.nn.sigmoid(acc)

    gr = gt_ref[...] + brow_ref[...]
    is_f_row = lax.broadcasted_iota(jnp.int32, gr.shape, 0) >= HEADS
    lf_r = jnp.where(is_f_row, jax.nn.log_sigmoid(gr), 0.0)
    gc = slab_ref[...] + bcol_ref[...]
    lane = lax.broadcasted_iota(jnp.int32, gc.shape, 1)
    is_f_col = (lane >= SLAB_F) & (lane < SLAB_F + HEADS)
    lf_c = jnp.where(is_f_col, jax.nn.log_sigmoid(gc), 0.0)

    row = lax.broadcasted_iota(jnp.int32, (CHUNK, CHUNK), 0)
    colid = lax.broadcasted_iota(jnp.int32, (CHUNK, CHUNK), 1)
    causal = colid <= row
    ones_col = (lax.broadcasted_iota(jnp.int32, (CHUNK, HV), 1) == 0).astype(BF16)

    for c in range(t // CHUNK):
        rows = slice(c * CHUNK, (c + 1) * CHUNK)
        bh_r = _split_dot(triu_ref[...], lf_r[:, rows], nt=True)
        bh_c = _split_dot(tril_ref[...], lf_c[rows, :], nt=False)
        a_r = gr[0:HEADS, rows] - bh_r[HEADS:, :]
        a_c = gc[rows, SLAB_I:SLAB_I + HEADS] - bh_c[:, SLAB_F:SLAB_F + HEADS]
        bh_c = bh_c[:, SLAB_F:SLAB_F + HEADS]
        for h in range(HEADS):
            qh = (qk[rows, h * QK:(h + 1) * QK] * (QK ** -0.5)).astype(BF16)
            kh = qk[rows, (HEADS + h) * QK:(HEADS + h + 1) * QK]
            vaug = jnp.concatenate([v_ref[rows, h * HV:(h + 1) * HV], ones_col], axis=-1)
            m_old = m_ref[h][0:1, 0:1]
            amat = jnp.where(causal, a_r[h:h + 1, :], -jnp.inf)
            big_m = jnp.maximum(jnp.max(amat, axis=-1, keepdims=True), m_old)
            dmat = jnp.exp(amat - big_m)
            s = _dot_nt(qh, kh.astype(BF16)) * dmat
            sc = jnp.exp(m_old - big_m)
            c_old = c_ref[h]
            tot = _dot(s.astype(BF16), vaug) + sc * _dot(qh, c_old.astype(BF16))
            num = tot[:, :HV]
            den = tot[:, HV:HV + 1]
            m_row = bh_c[:, h:h + 1] + big_m
            den = jnp.maximum(jnp.abs(den), jnp.exp(-m_row))
            hout = num / den
            m_last = big_m[CHUNK - 1:CHUNK, :]
            b_last = bh_c[CHUNK - 1:CHUNK, h:h + 1]
            wk = jnp.exp(a_c[:, h:h + 1] - m_last)
            dec = jnp.exp(m_old - m_last)
            c_ref[h] = dec * c_old + _dot_tn((kh * wk).astype(BF16), vaug)
            m_ref[h] = jnp.broadcast_to(b_last + m_last, m_ref.shape[1:])
            y = _rms(hout, gain_ref[:, h * HV:(h + 1) * HV])
            og = og_ref[rows, h * HV:(h + 1) * HV].astype(F32)
            o_ref[rows, h * HV:(h + 1) * HV] = (jax.nn.sigmoid(og) * y).astype(o_ref.dtype)


def _mlstm(z, slab, gt, conv_w, conv_b, b_i, b_f, gain, bsz, seq, t):
    n = bsz * seq
    nt = seq // t
    idx = jnp.arange(CHUNK)
    tril = (idx[None, :] <= idx[:, None]).astype(BF16)
    triu = tril.T
    brow = jnp.concatenate([b_i, b_f]).astype(F32).reshape(2 * HEADS, 1)
    bcol = jnp.zeros((1, SLAB_WIDTH), F32).at[0, SLAB_I:SLAB_I + 2 * HEADS].set(brow[:, 0])
    c2 = lambda b, i: (0, 0)

    def col(width, lane0):
        return pl.BlockSpec((t, width), lambda b, i: (b * nt + i, lane0 // width))

    return pl.pallas_call(
        _mlstm_kernel,
        grid=(bsz, nt),
        in_specs=[col(256, Z_MQ), col(256, Z_MK), col(512, Z_MV), col(512, Z_MO),
                  pl.BlockSpec((t, SLAB_WIDTH), lambda b, i: (b * nt + i, 0)),
                  pl.BlockSpec((2 * HEADS, t), lambda b, i: (0, b * nt + i)),
                  pl.BlockSpec((MLSTM_CONV, 2 * HEADS * QK), c2),
                  pl.BlockSpec((1, 2 * HEADS * QK), c2),
                  pl.BlockSpec((2 * HEADS, 1), c2),
                  pl.BlockSpec((1, SLAB_WIDTH), c2),
                  pl.BlockSpec((CHUNK, CHUNK), c2),
                  pl.BlockSpec((CHUNK, CHUNK), c2),
                  pl.BlockSpec((1, HEADS * HV), c2)],
        out_specs=pl.BlockSpec((t, HEADS * HV), lambda b, i: (b * nt + i, 0)),
        out_shape=jax.ShapeDtypeStruct((n, HEADS * HV), BF16),
        scratch_shapes=[pltpu.VMEM((t + TAIL, 2 * HEADS * QK), F32),
                        pltpu.VMEM((HEADS, QK, 2 * HV), F32),
                        pltpu.VMEM((HEADS, V7X_SUBLANES, V7X_LANES), F32)],
        compiler_params=_params(("arbitrary", "arbitrary"), 32),
        name="mlstm",
    )(z, z, z, z, slab, gt, conv_w.astype(F32), conv_b.astype(F32).reshape(1, -1), brow, bcol, tril, triu, gain)


VT_ROWS = HV + 16


def _mla_proj_kernel(cq_ref, ckv_ref, slab_ref, rope_ref, qn_ref, kvn_ref, wuq_ref, wuk_ref, wuvt_ref,
                     q_ref, k_ref, vt_ref):
    rope = rope_ref[...]
    cq = _rms(cq_ref[...].astype(F32), qn_ref[...]).astype(BF16)
    ckv = _rms(ckv_ref[...].astype(F32), kvn_ref[...]).astype(BF16)
    q = _dot(cq, wuq_ref[...])
    kn = _dot(ckv, wuk_ref[...])
    vt = _dot_nt(wuvt_ref[...], ckv)
    scale = (MLA_QK ** -0.5) * math.log2(math.e)
    q_rope = _apply_rope(q[:, HEADS * MLA_NOPE:], rope, HEADS)
    k_rope = _apply_rope(slab_ref[:, 0:QK], rope, 1)
    pad = VT_ROWS - HV
    ones_row = (lax.broadcasted_iota(jnp.int32, (pad, vt.shape[1]), 0) == 0).astype(BF16)
    for h in range(HEADS):
        q_ref[0, h, :, 0:MLA_NOPE] = (q[:, h * MLA_NOPE:(h + 1) * MLA_NOPE] * scale).astype(BF16)
        q_ref[0, h, :, MLA_NOPE:] = (q_rope[:, h * QK:(h + 1) * QK] * scale).astype(BF16)
        k_ref[0, h, :, 0:MLA_NOPE] = kn[:, h * MLA_NOPE:(h + 1) * MLA_NOPE].astype(BF16)
        k_ref[0, h, :, MLA_NOPE:] = k_rope.astype(BF16)
        vt_ref[0, h, 0:HV, :] = vt[h * HV:(h + 1) * HV, :].astype(BF16)
        vt_ref[0, h, HV:, :] = ones_row


def _mla_proj(z, slab, rope, q_norm, kv_norm, wuq, wuk, wuvt, bsz, seq, t):
    nt = seq // t
    c2 = lambda b, i: (0, 0)
    return pl.pallas_call(
        _mla_proj_kernel,
        grid=(bsz, nt),
        in_specs=[pl.BlockSpec((t, MLA_Q_RANK), lambda b, i: (b * nt + i, Z_CQ // MLA_Q_RANK)),
                  pl.BlockSpec((t, MLA_KV_RANK), lambda b, i: (b * nt + i, Z_CKV // MLA_KV_RANK)),
                  pl.BlockSpec((t, SLAB_WIDTH), lambda b, i: (b * nt + i, 0)),
                  pl.BlockSpec((t, 128), lambda b, i: (b * nt + i, 0)),
                  pl.BlockSpec((1, MLA_Q_RANK), c2),
                  pl.BlockSpec((1, MLA_KV_RANK), c2),
                  pl.BlockSpec((MLA_Q_RANK, HEADS * MLA_QK), c2),
                  pl.BlockSpec((MLA_KV_RANK, HEADS * MLA_NOPE), c2),
                  pl.BlockSpec((HEADS * HV, MLA_KV_RANK), c2)],
        out_specs=[pl.BlockSpec((1, HEADS, t, MLA_QK), lambda b, i: (b, 0, i, 0)),
                   pl.BlockSpec((1, HEADS, t, MLA_QK), lambda b, i: (b, 0, i, 0)),
                   pl.BlockSpec((1, HEADS, VT_ROWS, t), lambda b, i: (b, 0, 0, i))],
        out_shape=[jax.ShapeDtypeStruct((bsz, HEADS, seq, MLA_QK), BF16),
                   jax.ShapeDtypeStruct((bsz, HEADS, seq, MLA_QK), BF16),
                   jax.ShapeDtypeStruct((bsz, HEADS, VT_ROWS, seq), BF16)],
        compiler_params=_params(("parallel", "parallel"), 40),
        name="mla_proj",
    )(z, z, slab, rope, q_norm, kv_norm, wuq, wuk, wuvt)


FLASH_SUB = 1024


def _flash_kernel(q_ref, k_ref, vt_ref, gain_ref, o_ref, m_ref, acc_ref):
    qi = pl.program_id(1)
    ki = pl.program_id(2)
    tk = k_ref.shape[2]
    sub = min(FLASH_SUB, tk)

    @pl.when(ki == 0)
    def _():
        m_ref[...] = jnp.full_like(m_ref, -jnp.inf)
        acc_ref[...] = jnp.zeros_like(acc_ref)

    def step(diagonal):
        for h in range(HEADS):
            for c in range(tk // sub):
                c0 = c * sub if diagonal else 0
                keys = slice(c * sub, (c + 1) * sub)
                st = _dot_nt(k_ref[0, h, keys, :], q_ref[0, h, c0:, :])
                if diagonal:
                    key = c * sub + lax.broadcasted_iota(jnp.int32, st.shape, 0)
                    qry = c0 + lax.broadcasted_iota(jnp.int32, st.shape, 1)
                    st = jnp.where(key <= qry, st, -jnp.inf)
                m_old = m_ref[h, 0:1, c0:]
                m_new = jnp.maximum(m_old, jnp.max(st, axis=0, keepdims=True))
                alpha = jnp.exp2(m_old - m_new)
                pt = jnp.exp2(st - m_new).astype(BF16)
                acc_ref[h, :, c0:] = alpha * acc_ref[h, :, c0:] + _dot(vt_ref[0, h, :, keys], pt)
                m_ref[h, :, c0:] = jnp.broadcast_to(m_new, (V7X_SUBLANES, m_new.shape[1]))

    @pl.when(ki < qi)
    def _():
        step(False)

    @pl.when(ki == qi)
    def _():
        step(True)
        for h in range(HEADS):
            acc = acc_ref[h]
            yt = acc[0:HV, :] / acc[HV:HV + 1, :]
            yt = yt * lax.rsqrt(jnp.mean(yt * yt, axis=0, keepdims=True) + EPS)
            o_ref[:, h * HV:(h + 1) * HV] = (yt.T * gain_ref[:, h * HV:(h + 1) * HV]).astype(o_ref.dtype)


def _flash(q, k, vt, gain, t):
    bsz, _, seq, _ = q.shape
    nt = seq // t
    return pl.pallas_call(
        _flash_kernel,
        grid=(bsz, nt, nt),
        in_specs=[pl.BlockSpec((1, HEADS, t, MLA_QK), lambda b, i, j: (b, 0, i, 0)),
                  pl.BlockSpec((1, HEADS, t, MLA_QK), lambda b, i, j: (b, 0, jnp.minimum(i, j), 0)),
                  pl.BlockSpec((1, HEADS, VT_ROWS, t), lambda b, i, j: (b, 0, 0, jnp.minimum(i, j))),
                  pl.BlockSpec((1, HEADS * HV), lambda b, i, j: (0, 0))],
        out_specs=pl.BlockSpec((t, HEADS * HV), lambda b, i, j: (b * nt + i, 0)),
        out_shape=jax.ShapeDtypeStruct((bsz * seq, HEADS * HV), BF16),
        scratch_shapes=[pltpu.VMEM((HEADS, V7X_SUBLANES, t), F32),
                        pltpu.VMEM((HEADS, VT_ROWS, t), F32)],
        compiler_params=_params(("parallel", "parallel", "arbitrary"), 56),
        name="mla_flash",
    )(q, k, vt, gain)


def _outproj_kernel(h_ref, a_ref, b_ref, c_ref, d_ref, w_ref, g_ref, o_ref):
    y = _dot(a_ref[...], w_ref[0:512, :])
    y += _dot(b_ref[...], w_ref[512:1024, :])
    y += _dot(c_ref[...], w_ref[1024:1536, :])
    y += _dot(d_ref[...], w_ref[1536:2048, :])
    o_ref[...] = h_ref[...] + _rms(y, g_ref[...])


def _outproj(h, mixes, w, gain, tm):
    n, d = h.shape
    row = lambda i: (i, 0)
    vmem = (4 * tm * d * 4 + d * d * 2 + 8 * tm * 512 * 2 + 2 * tm * d * 4) / MIB + 6
    return pl.pallas_call(
        _outproj_kernel,
        grid=(n // tm,),
        in_specs=[pl.BlockSpec((tm, d), row)] + [pl.BlockSpec((tm, 512), row)] * 4 +
                 [pl.BlockSpec((d, d), lambda i: (0, 0), pipeline_mode=pl.Buffered(1)),
                  pl.BlockSpec((1, d), lambda i: (0, 0))],
        out_specs=pl.BlockSpec((tm, d), row),
        out_shape=jax.ShapeDtypeStruct((n, d), F32),
        compiler_params=_params(("parallel",), vmem),
        name="outproj",
    )(h, *mixes, w, gain)


def _ple_kernel(h_ref, p_ref, g6_ref, g7_ref, wg_ref, wp_ref, o_ref):
    h = h_ref[...]
    gate = jax.nn.sigmoid(_dot(_rms(h, g6_ref[...]).astype(BF16), wg_ref[...]))
    emb = _rms(_dot(p_ref[...].astype(BF16), wp_ref[...]), g7_ref[...])
    o_ref[...] = h + gate * emb


def _ple(h, p, g6, g7, wg, wp, tm):
    n, d = h.shape
    row = lambda i: (i, 0)
    c2 = lambda i: (0, 0)
    vmem = (4 * tm * d * 4 + d * d * 2 + 2 * PLE_DIM * d * 2 + 3 * tm * d * 4) / MIB + 6
    return pl.pallas_call(
        _ple_kernel,
        grid=(n // tm,),
        in_specs=[pl.BlockSpec((tm, d), row), pl.BlockSpec((tm, PLE_DIM), row),
                  pl.BlockSpec((1, d), c2), pl.BlockSpec((1, d), c2),
                  pl.BlockSpec((d, d), c2, pipeline_mode=pl.Buffered(1)),
                  pl.BlockSpec((PLE_DIM, d), c2)],
        out_specs=pl.BlockSpec((tm, d), row),
        out_shape=jax.ShapeDtypeStruct((n, d), F32),
        compiler_params=_params(("parallel",), vmem),
        name="ple",
    )(h, p, g6, g7, wg, wp)


def _permute_w_in(w_in):
    sizes = (SSM_WIDTH, 256, 256, 512, 512, 4, 4, MLA_Q_RANK, MLA_KV_RANK, QK, 256, 256, 512, 512)
    offs = np.concatenate([[0], np.cumsum(sizes)])
    (s_u, m_q, m_k, m_v, m_o, m_i, m_f, a_cq, a_ckv, a_kr, r_q, r_k, r_v, r_g) = [
        w_in[:, int(offs[i]):int(offs[i + 1])] for i in range(len(sizes))]
    pad = jnp.zeros((w_in.shape[0], SLAB_WIDTH - QK - 2 * HEADS), w_in.dtype)
    main = jnp.concatenate([s_u, m_v, m_o, r_v, r_g, m_q, m_k, a_ckv, r_q, r_k, a_cq,
                            a_kr, m_i, m_f, pad], axis=1).astype(BF16)
    gates_t = jnp.concatenate([m_i, m_f], axis=1).T.astype(BF16)
    return main, gates_t


def _permute_mla(w_uq, w_ukv):
    uq = w_uq.reshape(MLA_Q_RANK, HEADS, MLA_QK)
    uq = jnp.concatenate([uq[:, :, :MLA_NOPE].reshape(MLA_Q_RANK, -1),
                          uq[:, :, MLA_NOPE:].reshape(MLA_Q_RANK, -1)], axis=1)
    ukv = w_ukv.reshape(MLA_KV_RANK, HEADS, MLA_NOPE + HV)
    uk = ukv[:, :, :MLA_NOPE].reshape(MLA_KV_RANK, -1)
    uvt = ukv[:, :, MLA_NOPE:].reshape(MLA_KV_RANK, -1).T
    return uq.astype(BF16), uk.astype(BF16), uvt.astype(BF16)


def _tile(n, want):
    t = min(n, want)
    assert n % t == 0, (n, t)
    return t


def kernel(x, p, positions, norm_gains, ffn_w_gate, ffn_w_up, ffn_w_down, w_in, w_out, mix_norm_gain, ssm_a_re, ssm_a_im, ssm_b_re, ssm_b_im, ssm_c_re, ssm_c_im, ssm_d, ssm_log_dt, ssm_w_glu, ssm_b_glu, mlstm_conv_w, mlstm_conv_b, mlstm_b_i, mlstm_b_f, mla_q_norm, mla_kv_norm, mla_w_uq, mla_w_ukv, ple_w_proj, ple_w_gate):
    bsz, seq, d = x.shape
    n = bsz * seq
    depth = p.shape[0]
    assert d == D_MODEL and seq % CHUNK == 0

    tm_ffn = _tile(n, 1024)
    tf_ffn = _tile(D_FF, 512)
    tm_proj = _tile(n, 256)
    tm_row = _tile(n, 512)
    t_seq = _tile(seq, 512)
    t_s5 = _tile(seq, 512)
    t_att = _tile(seq, 2048)

    rope = _rope_table(positions, _tile(n, 2048))
    h = x.reshape(n, d).astype(F32)
    gains = norm_gains.astype(F32).reshape(depth, 8, 1, d)
    mixg = mix_norm_gain.astype(F32).reshape(depth, 4, 1, 512)
    wg_all, wu_all, wd_all = ffn_w_gate.astype(BF16), ffn_w_up.astype(BF16), ffn_w_down.astype(BF16)

    for i in range(depth):
        g = gains[i]
        h = _ffn(h, g[0], g[1], wg_all[i, 0], wu_all[i, 0], wd_all[i, 0], tm_ffn, tf_ffn)

        w_main, w_gt = _permute_w_in(w_in[i])
        z, slab, gt = _inproj(h, g[2], w_main, w_gt, tm_proj)

        wb, tabs, wc = _s5_tables(ssm_a_re[i], ssm_a_im[i], ssm_b_re[i], ssm_b_im[i],
                                  ssm_c_re[i], ssm_c_im[i], ssm_log_dt[i])
        out_a = _s5(z, wb, tabs, wc, ssm_d[i].astype(F32).reshape(1, -1), ssm_w_glu[i].astype(BF16),
                    ssm_b_glu[i].astype(F32).reshape(1, -1), mixg[i, 0], bsz, seq, t_s5)
        out_b = _mlstm(z, slab, gt, mlstm_conv_w[i], mlstm_conv_b[i], mlstm_b_i[i], mlstm_b_f[i],
                       mixg[i, 1], bsz, seq, t_seq)
        wuq, wuk, wuvt = _permute_mla(mla_w_uq[i], mla_w_ukv[i])
        q, k, vt = _mla_proj(z, slab, rope, mla_q_norm[i].astype(F32).reshape(1, -1),
                             mla_kv_norm[i].astype(F32).reshape(1, -1), wuq, wuk, wuvt, bsz, seq, t_seq)
        out_c = _flash(q, k, vt, mixg[i, 2], t_att)
        out_d = _retention(z, rope, mixg[i, 3], bsz, seq, t_seq)

        h = _outproj(h, (out_a, out_b, out_c, out_d), w_out[i].astype(BF16), g[3], tm_row)
        h = _ffn(h, g[4], g[5], wg_all[i, 1], wu_all[i, 1], wd_all[i, 1], tm_ffn, tf_ffn)
        h = _ple(h, p[i].reshape(n, PLE_DIM), g[6], g[7], ple_w_gate[i].astype(BF16),
                 ple_w_proj[i].astype(BF16), tm_row)
    return h.reshape(bsz, seq, d).astype(x.dtype)
```

```python
import functools
import math

import jax
import jax.numpy as jnp
import numpy as np
from jax import lax
from jax.experimental import pallas as pl
from jax.experimental.pallas import tpu as pltpu

F32 = jnp.float32
BF16 = jnp.bfloat16

EPS = 1e-6
ROPE_BASE = 10000.0

D_MODEL = 2048
PLE_DIM = 256
D_FF = 5632

SSM_WIDTH = 512
SSM_GROUP = 16
SSM_GROUPS = SSM_WIDTH // SSM_GROUP
SSM_STATE = 64
SSM_LANES = SSM_GROUPS * SSM_STATE

HEADS = 4
QK = 64
HV = 128
CHUNK = 256
MLSTM_CONV = 4
MLA_Q_RANK = 384
MLA_KV_RANK = 256
MLA_NOPE = 128
MLA_QK = MLA_NOPE + QK

V7X_LANES = 128
V7X_SUBLANES = 8
V7X_VMEM_BYTES = 64 * 1024 * 1024
MIB = 1024 * 1024

Z_SU, Z_MV, Z_MO, Z_RV, Z_RG = 0, 512, 1024, 1536, 2048
Z_MQ, Z_MK, Z_CKV, Z_RQ, Z_RK = 2560, 2816, 3072, 3328, 3584
Z_CQ = 3840
Z_WIDTH = 4224
SLAB_WIDTH = 128
SLAB_I, SLAB_F = 64, 68


def _params(sem, vmem_mib):
    return pltpu.CompilerParams(dimension_semantics=sem, vmem_limit_bytes=int(vmem_mib * MIB))


def _rms(x, gain):
    ms = jnp.mean(x * x, axis=-1, keepdims=True)
    return x * lax.rsqrt(ms + EPS) * gain


def _dot(a, b):
    return jnp.dot(a, b, preferred_element_type=F32)


def _dot_nt(a, b):
    return lax.dot_general(a, b, (((1,), (1,)), ((), ())), preferred_element_type=F32)


def _dot_tn(a, b):
    return lax.dot_general(a, b, (((0,), (0,)), ((), ())), preferred_element_type=F32)


def _split_dot(tri, x, nt):
    hi = x.astype(BF16)
    lo = (x - hi.astype(F32)).astype(BF16)
    if nt:
        return _dot(hi, tri) + _dot(lo, tri)
    return _dot(tri, hi) + _dot(tri, lo)


def _rope_kernel(pos_ref, inv_ref, o_ref):
    ang = pos_ref[...].astype(F32) * inv_ref[...]
    lane = lax.broadcasted_iota(jnp.int32, ang.shape, 1)
    sgn = jnp.where(lane < 96, -1.0, 1.0).astype(F32)
    o_ref[...] = jnp.where(lane < 64, jnp.cos(ang), sgn * jnp.sin(ang))


def _rope_table(positions, tm):
    n = positions.size
    inv = 1.0 / (ROPE_BASE ** (jnp.arange(0, QK, 2, dtype=F32) / QK))
    inv = jnp.tile(inv, 4).reshape(1, 128)
    return pl.pallas_call(
        _rope_kernel,
        grid=(n // tm,),
        in_specs=[pl.BlockSpec((tm, 1), lambda i: (i, 0)),
                  pl.BlockSpec((1, 128), lambda i: (0, 0))],
        out_specs=pl.BlockSpec((tm, 128), lambda i: (i, 0)),
        out_shape=jax.ShapeDtypeStruct((n, 128), F32),
        compiler_params=_params(("parallel",), 32),
        name="rope_table",
    )(positions.reshape(n, 1), inv)


def _apply_rope(x, rope, nheads):
    width = nheads * QK
    cos = rope[:, :QK]
    sin = rope[:, QK:]
    if nheads > 1:
        cos = jnp.concatenate([cos] * nheads, axis=-1)
        sin = jnp.concatenate([sin] * nheads, axis=-1)
    if width >= V7X_LANES:
        lane = lax.broadcasted_iota(jnp.int32, x.shape, 1)
        first = (lane % QK) < (QK // 2)
        rot = jnp.where(first, pltpu.roll(x, width - QK // 2, 1), pltpu.roll(x, QK // 2, 1))
    else:
        rot = jnp.concatenate([x[:, QK // 2:], x[:, :QK // 2]], axis=-1)
    return x * cos + rot * sin


def _ffn_kernel(h_ref, gpre_ref, gpost_ref, wg_ref, wu_ref, wd_ref, o_ref, xn_ref):
    j = pl.program_id(1)

    @pl.when(j == 0)
    def _():
        xn_ref[...] = _rms(h_ref[...], gpre_ref[...]).astype(BF16)
        o_ref[...] = jnp.zeros_like(o_ref)

    x = xn_ref[...]
    g = _dot(x, wg_ref[...])
    u = _dot(x, wu_ref[...])
    a = (g * jax.nn.sigmoid(g) * u).astype(BF16)
    o_ref[...] += _dot(a, wd_ref[...])

    @pl.when(j == pl.num_programs(1) - 1)
    def _():
        o_ref[...] = h_ref[...] + _rms(o_ref[...], gpost_ref[...])


def _ffn(h, gpre, gpost, wg, wu, wd, tm, tf):
    n, d = h.shape
    f = wg.shape[1]
    vmem = (4 * tm * d * 4 + tm * d * 2 + 6 * d * tf * 2 + 6 * tm * tf * 4) / MIB + 4
    return pl.pallas_call(
        _ffn_kernel,
        grid=(n // tm, f // tf),
        in_specs=[pl.BlockSpec((tm, d), lambda i, j: (i, 0), pipeline_mode=pl.Buffered(1)),
                  pl.BlockSpec((1, d), lambda i, j: (0, 0)),
                  pl.BlockSpec((1, d), lambda i, j: (0, 0)),
                  pl.BlockSpec((d, tf), lambda i, j: (0, j)),
                  pl.BlockSpec((d, tf), lambda i, j: (0, j)),
                  pl.BlockSpec((tf, d), lambda i, j: (j, 0))],
        out_specs=pl.BlockSpec((tm, d), lambda i, j: (i, 0)),
        out_shape=jax.ShapeDtypeStruct((n, d), F32),
        scratch_shapes=[pltpu.VMEM((tm, d), BF16)],
        compiler_params=_params(("parallel", "arbitrary"), vmem),
        name="ffn",
    )(h, gpre, 0.5 * gpost, wg, wu, wd)


def _inproj_kernel(h_ref, g_ref, w_ref, wgt_ref, z_ref, slab_ref, gt_ref):
    xn = _rms(h_ref[...], g_ref[...]).astype(BF16)
    z = _dot(xn, w_ref[...])
    z_ref[...] = z[:, :Z_WIDTH].astype(BF16)
    slab_ref[...] = z[:, Z_WIDTH:]
    gt_ref[...] = _dot_nt(wgt_ref[...], xn)


def _inproj(h, gain, w, wgt, tm):
    n, d = h.shape
    wtot = Z_WIDTH + SLAB_WIDTH
    vmem = (2 * tm * d * 4 + d * wtot * 2 + 2 * tm * wtot * 4 + tm * wtot * 4) / MIB + 6
    return pl.pallas_call(
        _inproj_kernel,
        grid=(n // tm,),
        in_specs=[pl.BlockSpec((tm, d), lambda i: (i, 0)),
                  pl.BlockSpec((1, d), lambda i: (0, 0)),
                  pl.BlockSpec((d, wtot), lambda i: (0, 0), pipeline_mode=pl.Buffered(1)),
                  pl.BlockSpec((8, d), lambda i: (0, 0))],
        out_specs=[pl.BlockSpec((tm, Z_WIDTH), lambda i: (i, 0)),
                   pl.BlockSpec((tm, SLAB_WIDTH), lambda i: (i, 0)),
                   pl.BlockSpec((8, tm), lambda i: (0, i))],
        out_shape=[jax.ShapeDtypeStruct((n, Z_WIDTH), BF16),
                   jax.ShapeDtypeStruct((n, SLAB_WIDTH), F32),
                   jax.ShapeDtypeStruct((8, n), F32)],
        compiler_params=_params(("parallel",), vmem),
        name="inproj",
    )(h, gain, w, wgt)


SCAN_COLS = 512
S5_HALVES = 2
S5_SUB = 256


def _s5_kernel(u_ref, wb_ref, wba_ref, tab_ref, wc_ref, d_ref, wglu_ref, bglu_ref, gain_ref,
               o_ref, x_ref, carry_ref):
    t = u_ref.shape[0]
    hl = SSM_LANES // S5_HALVES
    hw = SSM_WIDTH // S5_HALVES

    @pl.when(pl.program_id(1) == 0)
    def _():
        carry_ref[...] = jnp.zeros_like(carry_ref)

    lanes = []
    for c0 in range(0, SSM_LANES, SCAN_COLS):
        k, off = divmod(c0, hl)
        lanes.append((slice(2 * hl * k + off, 2 * hl * k + off + SCAN_COLS),
                      slice(2 * hl * k + hl + off, 2 * hl * k + hl + off + SCAN_COLS),
                      slice(c0, c0 + SCAN_COLS)))
    last = V7X_SUBLANES - 1

    def project_in(rows):
        u = u_ref[rows, :]
        uf = u.astype(F32)
        first = lax.broadcasted_iota(jnp.int32, uf.shape, 0) % V7X_SUBLANES == 0
        prev = jnp.where(first, 0.0, pltpu.roll(uf, 1, 0)).astype(BF16)
        for k in range(S5_HALVES):
            cols = slice(hw * k, hw * (k + 1))
            x_ref[rows, 2 * hl * k:2 * hl * (k + 1)] = _dot(u[:, cols], wb_ref[k]) + _dot(prev[:, cols], wba_ref[k])

    def scan(row0, nrows):
        for re, im, tab in lanes:
            cr, ci = carry_ref[:, re], carry_ref[:, im]
            for r0 in range(row0, row0 + nrows, V7X_SUBLANES):
                rows = slice(r0, r0 + V7X_SUBLANES)
                xr = x_ref[rows, re]
                xi = x_ref[rows, im]
                for j, shift in ((1, 2), (2, 4)):
                    ar, ai = tab_ref[2 * j, :, tab], tab_ref[2 * j + 1, :, tab]
                    rr = pltpu.roll(xr, shift, 0)
                    ri = pltpu.roll(xi, shift, 0)
                    xr, xi = xr + ar * rr - ai * ri, xi + ar * ri + ai * rr
                pr, pi = tab_ref[6, :, tab], tab_ref[7, :, tab]
                xr, xi = xr + pr * cr - pi * ci, xi + pr * ci + pi * cr
                x_ref[rows, re] = xr
                x_ref[rows, im] = xi
                cr, ci = jnp.broadcast_to(xr[last:, :], xr.shape), jnp.broadcast_to(xi[last:, :], xi.shape)
            carry_ref[:, re] = cr
            carry_ref[:, im] = ci

    def project_out(rows):
        y = jnp.concatenate([_dot(x_ref[rows, 2 * hl * k:2 * hl * (k + 1)].astype(BF16), wc_ref[k])
                             for k in range(S5_HALVES)], axis=-1)
        y = y + d_ref[...] * u_ref[rows, :].astype(F32)
        g = jax.nn.gelu(y)
        out = g * jax.nn.sigmoid(_dot(g.astype(BF16), wglu_ref[...]) + bglu_ref[...])
        o_ref[rows, :] = _rms(out, gain_ref[...]).astype(o_ref.dtype)

    sub = min(S5_SUB, t)
    for r0 in range(0, t, sub):
        project_in(slice(r0, r0 + sub))
    for r0 in range(0, t, sub):
        scan(r0, sub)
        project_out(slice(r0, r0 + sub))


def _s5(z, wb, wba, tabs, wc, d_skip, wglu, bglu, gain, bsz, seq, t):
    n = bsz * seq
    nt = seq // t
    nl2 = 2 * SSM_LANES
    const = lambda b, i: (0, 0)
    vmem = (t * nl2 * 4 * 3 + 2 * SSM_WIDTH * nl2 * 2 * 2 + 8 * 8 * SSM_LANES * 4 * 2) / MIB + 8
    return pl.pallas_call(
        _s5_kernel,
        grid=(bsz, nt),
        in_specs=[pl.BlockSpec((t, SSM_WIDTH), lambda b, i: (b * nt + i, Z_SU // SSM_WIDTH)),
                  pl.BlockSpec((S5_HALVES, SSM_WIDTH // S5_HALVES, nl2 // S5_HALVES), lambda b, i: (0, 0, 0)),
                  pl.BlockSpec((S5_HALVES, SSM_WIDTH // S5_HALVES, nl2 // S5_HALVES), lambda b, i: (0, 0, 0)),
                  pl.BlockSpec((8, V7X_SUBLANES, SSM_LANES), lambda b, i: (0, 0, 0)),
                  pl.BlockSpec((S5_HALVES, nl2 // S5_HALVES, SSM_WIDTH // S5_HALVES), lambda b, i: (0, 0, 0)),
                  pl.BlockSpec((1, SSM_WIDTH), const),
                  pl.BlockSpec((SSM_WIDTH, SSM_WIDTH), const),
                  pl.BlockSpec((1, SSM_WIDTH), const),
                  pl.BlockSpec((1, SSM_WIDTH), const)],
        out_specs=pl.BlockSpec((t, SSM_WIDTH), lambda b, i: (b * nt + i, 0)),
        out_shape=jax.ShapeDtypeStruct((n, SSM_WIDTH), BF16),
        scratch_shapes=[pltpu.VMEM((t, nl2), F32), pltpu.VMEM((V7X_SUBLANES, nl2), F32)],
        compiler_params=_params(("arbitrary", "arbitrary"), vmem),
        name="s5",
    )(z, wb, wba, tabs, wc, d_skip, wglu, bglu, gain)


def _s5_tables(a_re, a_im, b_re, b_im, c_re, c_im, log_dt):
    lr, li = a_re.astype(F32), a_im.astype(F32)
    dt = jnp.exp(log_dt.astype(F32))[:, None]
    mag = jnp.exp(lr * dt)
    ar, ai = mag * jnp.cos(li * dt), mag * jnp.sin(li * dt)
    den = lr * lr + li * li
    nr, ni = ar - 1.0, ai
    zr = (nr * lr + ni * li) / den
    zi = (ni * lr - nr * li) / den
    br_, bi_ = b_re.astype(F32), b_im.astype(F32)
    bbr = zr[..., None] * br_ - zi[..., None] * bi_
    bbi = zr[..., None] * bi_ + zi[..., None] * br_
    eye = jnp.eye(SSM_GROUPS, dtype=F32)
    wb_re = jnp.einsum('gnc,gh->gchn', bbr, eye).reshape(SSM_WIDTH, SSM_LANES)
    wb_im = jnp.einsum('gnc,gh->gchn', bbi, eye).reshape(SSM_WIDTH, SSM_LANES)
    hw, hl = SSM_WIDTH // S5_HALVES, SSM_LANES // S5_HALVES

    def halves_in(w_re, w_im):
        return jnp.stack([jnp.concatenate([w[hw * k:hw * (k + 1), hl * k:hl * (k + 1)] for w in (w_re, w_im)], axis=1)
                          for k in range(S5_HALVES)]).astype(BF16)

    ar_l, ai_l = ar.reshape(1, SSM_LANES), ai.reshape(1, SSM_LANES)
    wb = halves_in(wb_re, wb_im)
    wba = halves_in(wb_re * ar_l - wb_im * ai_l, wb_re * ai_l + wb_im * ar_l)
    wc_re = jnp.einsum('gcn,gh->gnhc', c_re.astype(F32), eye).reshape(SSM_LANES, SSM_WIDTH)
    wc_im = jnp.einsum('gcn,gh->gnhc', c_im.astype(F32), eye).reshape(SSM_LANES, SSM_WIDTH)
    wc = jnp.stack([jnp.concatenate([w[hl * k:hl * (k + 1), hw * k:hw * (k + 1)] for w in (wc_re, -wc_im)], axis=0)
                    for k in range(S5_HALVES)]).astype(BF16)

    a1 = (ar.reshape(1, SSM_LANES), ai.reshape(1, SSM_LANES))

    def cmul(p, q):
        return (p[0] * q[0] - p[1] * q[1], p[0] * q[1] + p[1] * q[0])

    pows = [a1]
    for _ in range(7):
        pows.append(cmul(pows[-1], a1))
    row = jnp.arange(V7X_SUBLANES)[:, None]
    tabs = []
    for shift in (1, 2, 4):
        pr, pi = pows[shift - 1]
        keep = row >= shift
        tabs += [jnp.where(keep, pr, 0.0), jnp.where(keep, pi, 0.0)]
    tabs += [jnp.concatenate([p[0] for p in pows], axis=0), jnp.concatenate([p[1] for p in pows], axis=0)]
    return wb, wba, jnp.stack(tabs).astype(F32), wc


def _ret_kernel(q_ref, k_ref, v_ref, g_ref, rope_ref, dmask_ref, qdec_ref, kdec_ref, cdec_ref, gain_ref,
                o_ref, state_ref):
    t = q_ref.shape[0]

    @pl.when(pl.program_id(1) == 0)
    def _():
        state_ref[...] = jnp.zeros_like(state_ref)

    for c in range(t // CHUNK):
        rows = slice(c * CHUNK, (c + 1) * CHUNK)
        rope = rope_ref[rows, :]
        q = _apply_rope(q_ref[rows, :].astype(F32), rope, HEADS)
        k = _apply_rope(k_ref[rows, :].astype(F32), rope, HEADS) * (QK ** -0.5)
        for h in range(HEADS):
            qh = q[:, h * QK:(h + 1) * QK].astype(BF16)
            kh = k[:, h * QK:(h + 1) * QK]
            vh = v_ref[rows, h * HV:(h + 1) * HV]
            s = _dot_nt(qh, kh.astype(BF16)) * dmask_ref[h]
            inner = _dot(s.astype(BF16), vh)
            r_old = state_ref[h]
            cross = _dot(qh, r_old.astype(BF16)) * qdec_ref[h]
            state_ref[h] = cdec_ref[h] * r_old + _dot_tn((kh * kdec_ref[h]).astype(BF16), vh)
            y = _rms(inner + cross, gain_ref[:, h * HV:(h + 1) * HV])
            gate = g_ref[rows, h * HV:(h + 1) * HV].astype(F32)
            o_ref[rows, h * HV:(h + 1) * HV] = (gate * jax.nn.sigmoid(gate) * y).astype(o_ref.dtype)


def _ret_tables():
    log_g = jnp.log1p(-jnp.exp2(-5.0 - jnp.arange(HEADS, dtype=F32)))
    idx = jnp.arange(CHUNK)
    diff = idx[:, None] - idx[None, :]
    causal = diff >= 0
    dmask = jnp.where(causal, jnp.exp(jnp.where(causal, diff, 0).astype(F32) * log_g[:, None, None]), 0.0)
    q_decay = jnp.exp((idx + 1).astype(F32)[None, :] * log_g[:, None])
    k_decay = jnp.exp((CHUNK - 1 - idx).astype(F32)[None, :] * log_g[:, None])
    chunk_decay = jnp.exp(CHUNK * log_g)
    qdec = jnp.broadcast_to(q_decay[:, :, None], (HEADS, CHUNK, HV))
    kdec = jnp.broadcast_to(k_decay[:, :, None], (HEADS, CHUNK, QK))
    cdec = jnp.broadcast_to(chunk_decay[:, None, None], (HEADS, QK, HV))
    return dmask, qdec, kdec, cdec


def _retention(z, rope, gain, bsz, seq, t):
    n = bsz * seq
    nt = seq // t
    dmask, qdec, kdec, cdec = _ret_tables()
    c3 = lambda b, i: (0, 0, 0)

    def col(width, lane0):
        return pl.BlockSpec((t, width), lambda b, i: (b * nt + i, lane0 // width))

    return pl.pallas_call(
        _ret_kernel,
        grid=(bsz, nt),
        in_specs=[col(256, Z_RQ), col(256, Z_RK), col(512, Z_RV), col(512, Z_RG),
                  pl.BlockSpec((t, 128), lambda b, i: (b * nt + i, 0)),
                  pl.BlockSpec((HEADS, CHUNK, CHUNK), c3),
                  pl.BlockSpec((HEADS, CHUNK, HV), c3),
                  pl.BlockSpec((HEADS, CHUNK, QK), c3),
                  pl.BlockSpec((HEADS, QK, HV), c3),
                  pl.BlockSpec((1, HEADS * HV), lambda b, i: (0, 0))],
        out_specs=pl.BlockSpec((t, HEADS * HV), lambda b, i: (b * nt + i, 0)),
        out_shape=jax.ShapeDtypeStruct((n, HEADS * HV), BF16),
        scratch_shapes=[pltpu.VMEM((HEADS, QK, HV), F32)],
        compiler_params=_params(("arbitrary", "arbitrary"), 32),
        name="retention",
    )(z, z, z, z, rope, dmask, qdec, kdec, cdec, gain)


TAIL = V7X_SUBLANES


def _mlstm_kernel(q_ref, k_ref, v_ref, og_ref, slab_ref, gt_ref, cw_ref, cb_ref, brow_ref, bcol_ref,
                  tril_ref, triu_ref, gain_ref, o_ref, xbuf_ref, c_ref, m_ref):
    t = q_ref.shape[0]
    width = 2 * HEADS * QK

    @pl.when(pl.program_id(1) == 0)
    def _():
        xbuf_ref[0:TAIL, :] = jnp.zeros((TAIL, width), F32)
        c_ref[...] = jnp.zeros_like(c_ref)
        m_ref[...] = jnp.zeros_like(m_ref)

    xbuf_ref[TAIL:, 0:HEADS * QK] = q_ref[...].astype(F32)
    xbuf_ref[TAIL:, HEADS * QK:] = k_ref[...].astype(F32)
    acc = cb_ref[...] + cw_ref[MLSTM_CONV - 1:MLSTM_CONV, :] * xbuf_ref[TAIL:, :]
    for j in range(MLSTM_CONV - 1):
        off = TAIL - (MLSTM_CONV - 1) + j
        acc = acc + cw_ref[j:j + 1, :] * xbuf_ref[off:off + t, :]
    xbuf_ref[0:TAIL, :] = xbuf_ref[t:t + TAIL, :]
    qk = acc * jax.nn.sigmoid(acc)

    gr = gt_ref[...] + brow_ref[...]
    is_f_row = lax.broadcasted_iota(jnp.int32, gr.shape, 0) >= HEADS
    lf_r = jnp.where(is_f_row, jax.nn.log_sigmoid(gr), 0.0)
    gc = slab_ref[...] + bcol_ref[...]
    lane = lax.broadcasted_iota(jnp.int32, gc.shape, 1)
    is_f_col = (lane >= SLAB_F) & (lane < SLAB_F + HEADS)
    lf_c = jnp.where(is_f_col, jax.nn.log_sigmoid(gc), 0.0)

    row = lax.broadcasted_iota(jnp.int32, (CHUNK, CHUNK), 0)
    colid = lax.broadcasted_iota(jnp.int32, (CHUNK, CHUNK), 1)
    causal = colid <= row
    ones_col = (lax.broadcasted_iota(jnp.int32, (CHUNK, HV), 1) == 0).astype(BF16)

    for c in range(t // CHUNK):
        rows = slice(c * CHUNK, (c + 1) * CHUNK)
        bh_r = _split_dot(triu_ref[...], lf_r[:, rows], nt=True)
        bh_c = _split_dot(tril_ref[...], lf_c[rows, :], nt=False)
        a_r = gr[0:HEADS, rows] - bh_r[HEADS:, :]
        a_c = gc[rows, SLAB_I:SLAB_I + HEADS] - bh_c[:, SLAB_F:SLAB_F + HEADS]
        bh_c = bh_c[:, SLAB_F:SLAB_F + HEADS]
        hs = range(HEADS)
        qh = [(qk[rows, h * QK:(h + 1) * QK] * (QK ** -0.5)).astype(BF16) for h in hs]
        kh = [qk[rows, (HEADS + h) * QK:(HEADS + h + 1) * QK] for h in hs]
        vaug = [jnp.concatenate([v_ref[rows, h * HV:(h + 1) * HV], ones_col], axis=-1) for h in hs]
        m_old = [m_ref[h][0:1, 0:1] for h in hs]
        c_old = [c_ref[h] for h in hs]
        qk_raw = [_dot_nt(qh[h], kh[h].astype(BF16)) for h in hs]
        q_c = [_dot(qh[h], c_old[h].astype(BF16)) for h in hs]
        amat = [jnp.where(causal, a_r[h:h + 1, :], -jnp.inf) for h in hs]
        big_m = [jnp.maximum(jnp.max(amat[h], axis=-1, keepdims=True), m_old[h]) for h in hs]
        s = [(qk_raw[h] * jnp.exp(amat[h] - big_m[h])).astype(BF16) for h in hs]
        sc = [jnp.exp(m_old[h] - big_m[h]) for h in hs]
        tot = [_dot(s[h], vaug[h]) + sc[h] * q_c[h] for h in hs]
        m_last = [big_m[h][CHUNK - 1:CHUNK, :] for h in hs]
        wk = [jnp.exp(a_c[:, h:h + 1] - m_last[h]) for h in hs]
        upd = [_dot_tn((kh[h] * wk[h]).astype(BF16), vaug[h]) for h in hs]
        for h in hs:
            c_ref[h] = jnp.exp(m_old[h] - m_last[h]) * c_old[h] + upd[h]
            m_ref[h] = jnp.broadcast_to(bh_c[CHUNK - 1:CHUNK, h:h + 1] + m_last[h], m_ref.shape[1:])
        for h in hs:
            den = jnp.maximum(jnp.abs(tot[h][:, HV:HV + 1]), jnp.exp(-(bh_c[:, h:h + 1] + big_m[h])))
            y = _rms(tot[h][:, :HV] / den, gain_ref[:, h * HV:(h + 1) * HV])
            og = og_ref[rows, h * HV:(h + 1) * HV].astype(F32)
            o_ref[rows, h * HV:(h + 1) * HV] = (jax.nn.sigmoid(og) * y).astype(o_ref.dtype)


def _mlstm(z, slab, gt, conv_w, conv_b, b_i, b_f, gain, bsz, seq, t):
    n = bsz * seq
    nt = seq // t
    idx = jnp.arange(CHUNK)
    tril = (idx[None, :] <= idx[:, None]).astype(BF16)
    triu = tril.T
    brow = jnp.concatenate([b_i, b_f]).astype(F32).reshape(2 * HEADS, 1)
    bcol = jnp.zeros((1, SLAB_WIDTH), F32).at[0, SLAB_I:SLAB_I + 2 * HEADS].set(brow[:, 0])
    c2 = lambda b, i: (0, 0)

    def col(width, lane0):
        return pl.BlockSpec((t, width), lambda b, i: (b * nt + i, lane0 // width))

    return pl.pallas_call(
        _mlstm_kernel,
        grid=(bsz, nt),
        in_specs=[col(256, Z_MQ), col(256, Z_MK), col(512, Z_MV), col(512, Z_MO),
                  pl.BlockSpec((t, SLAB_WIDTH), lambda b, i: (b * nt + i, 0)),
                  pl.BlockSpec((2 * HEADS, t), lambda b, i: (0, b * nt + i)),
                  pl.BlockSpec((MLSTM_CONV, 2 * HEADS * QK), c2),
                  pl.BlockSpec((1, 2 * HEADS * QK), c2),
                  pl.BlockSpec((2 * HEADS, 1), c2),
                  pl.BlockSpec((1, SLAB_WIDTH), c2),
                  pl.BlockSpec((CHUNK, CHUNK), c2),
                  pl.BlockSpec((CHUNK, CHUNK), c2),
                  pl.BlockSpec((1, HEADS * HV), c2)],
        out_specs=pl.BlockSpec((t, HEADS * HV), lambda b, i: (b * nt + i, 0)),
        out_shape=jax.ShapeDtypeStruct((n, HEADS * HV), BF16),
        scratch_shapes=[pltpu.VMEM((t + TAIL, 2 * HEADS * QK), F32),
                        pltpu.VMEM((HEADS, QK, 2 * HV), F32),
                        pltpu.VMEM((HEADS, V7X_SUBLANES, V7X_LANES), F32)],
        compiler_params=_params(("arbitrary", "arbitrary"), 32),
        name="mlstm",
    )(z, z, z, z, slab, gt, conv_w.astype(F32), conv_b.astype(F32).reshape(1, -1), brow, bcol, tril, triu, gain)


VT_ROWS = HV + 16


def _mla_proj_kernel(cq_ref, ckv_ref, slab_ref, rope_ref, qn_ref, kvn_ref, wuq_ref, wuk_ref, wuvt_ref,
                     q_ref, k_ref, vt_ref):
    rope = rope_ref[...]
    cq = _rms(cq_ref[...].astype(F32), qn_ref[...]).astype(BF16)
    ckv = _rms(ckv_ref[...].astype(F32), kvn_ref[...]).astype(BF16)
    q = _dot(cq, wuq_ref[...])
    kn = _dot(ckv, wuk_ref[...])
    vt = _dot_nt(wuvt_ref[...], ckv)
    scale = (MLA_QK ** -0.5) * math.log2(math.e)
    q_rope = _apply_rope(q[:, HEADS * MLA_NOPE:], rope, HEADS)
    k_rope = _apply_rope(slab_ref[:, 0:QK], rope, 1)
    pad = VT_ROWS - HV
    ones_row = (lax.broadcasted_iota(jnp.int32, (pad, vt.shape[1]), 0) == 0).astype(BF16)
    for h in range(HEADS):
        q_ref[0, h, :, 0:MLA_NOPE] = (q[:, h * MLA_NOPE:(h + 1) * MLA_NOPE] * scale).astype(BF16)
        q_ref[0, h, :, MLA_NOPE:] = (q_rope[:, h * QK:(h + 1) * QK] * scale).astype(BF16)
        k_ref[0, h, :, 0:MLA_NOPE] = kn[:, h * MLA_NOPE:(h + 1) * MLA_NOPE].astype(BF16)
        k_ref[0, h, :, MLA_NOPE:] = k_rope.astype(BF16)
        vt_ref[0, h, 0:HV, :] = vt[h * HV:(h + 1) * HV, :].astype(BF16)
        vt_ref[0, h, HV:, :] = ones_row


def _mla_proj(z, slab, rope, q_norm, kv_norm, wuq, wuk, wuvt, bsz, seq, t):
    nt = seq // t
    c2 = lambda b, i: (0, 0)
    return pl.pallas_call(
        _mla_proj_kernel,
        grid=(bsz, nt),
        in_specs=[pl.BlockSpec((t, MLA_Q_RANK), lambda b, i: (b * nt + i, Z_CQ // MLA_Q_RANK)),
                  pl.BlockSpec((t, MLA_KV_RANK), lambda b, i: (b * nt + i, Z_CKV // MLA_KV_RANK)),
                  pl.BlockSpec((t, SLAB_WIDTH), lambda b, i: (b * nt + i, 0)),
                  pl.BlockSpec((t, 128), lambda b, i: (b * nt + i, 0)),
                  pl.BlockSpec((1, MLA_Q_RANK), c2),
                  pl.BlockSpec((1, MLA_KV_RANK), c2),
                  pl.BlockSpec((MLA_Q_RANK, HEADS * MLA_QK), c2),
                  pl.BlockSpec((MLA_KV_RANK, HEADS * MLA_NOPE), c2),
                  pl.BlockSpec((HEADS * HV, MLA_KV_RANK), c2)],
        out_specs=[pl.BlockSpec((1, HEADS, t, MLA_QK), lambda b, i: (b, 0, i, 0)),
                   pl.BlockSpec((1, HEADS, t, MLA_QK), lambda b, i: (b, 0, i, 0)),
                   pl.BlockSpec((1, HEADS, VT_ROWS, t), lambda b, i: (b, 0, 0, i))],
        out_shape=[jax.ShapeDtypeStruct((bsz, HEADS, seq, MLA_QK), BF16),
                   jax.ShapeDtypeStruct((bsz, HEADS, seq, MLA_QK), BF16),
                   jax.ShapeDtypeStruct((bsz, HEADS, VT_ROWS, seq), BF16)],
        compiler_params=_params(("parallel", "parallel"), 40),
        name="mla_proj",
    )(z, z, slab, rope, q_norm, kv_norm, wuq, wuk, wuvt)


FLASH_SUB = 1024


def _flash_kernel(q_ref, k_ref, vt_ref, gain_ref, o_ref, m_ref, acc_ref):
    qi = pl.program_id(1)
    ki = pl.program_id(2)
    tk = k_ref.shape[2]
    sub = min(FLASH_SUB, tk)

    @pl.when(ki == 0)
    def _():
        m_ref[...] = jnp.full_like(m_ref, -jnp.inf)
        acc_ref[...] = jnp.zeros_like(acc_ref)

    def step(diagonal):
        chains = [(h, c) for h in range(HEADS) for c in range(tk // sub)]

        def scores(h, c):
            c0 = c * sub if diagonal else 0
            st = _dot_nt(k_ref[0, h, c * sub:(c + 1) * sub, :], q_ref[0, h, c0:, :])
            if diagonal:
                key = c * sub + lax.broadcasted_iota(jnp.int32, st.shape, 0)
                qry = c0 + lax.broadcasted_iota(jnp.int32, st.shape, 1)
                st = jnp.where(key <= qry, st, -jnp.inf)
            return st

        st_next = scores(*chains[0])
        for idx, (h, c) in enumerate(chains):
            st = st_next
            if idx + 1 < len(chains):
                st_next = scores(*chains[idx + 1])
            c0 = c * sub if diagonal else 0
            m_old = m_ref[h, 0:1, c0:]
            m_new = jnp.maximum(m_old, jnp.max(st, axis=0, keepdims=True))
            alpha = jnp.exp2(m_old - m_new)
            pt = jnp.exp2(st - m_new).astype(BF16)
            acc_ref[h, :, c0:] = alpha * acc_ref[h, :, c0:] + _dot(vt_ref[0, h, :, c * sub:(c + 1) * sub], pt)
            m_ref[h, :, c0:] = jnp.broadcast_to(m_new, (V7X_SUBLANES, m_new.shape[1]))

    @pl.when(ki < qi)
    def _():
        step(False)

    @pl.when(ki == qi)
    def _():
        step(True)
        for h in range(HEADS):
            acc = acc_ref[h]
            yt = acc[0:HV, :] / acc[HV:HV + 1, :]
            yt = yt * lax.rsqrt(jnp.mean(yt * yt, axis=0, keepdims=True) + EPS)
            o_ref[:, h * HV:(h + 1) * HV] = (yt.T * gain_ref[:, h * HV:(h + 1) * HV]).astype(o_ref.dtype)


def _flash(q, k, vt, gain, t):
    bsz, _, seq, _ = q.shape
    nt = seq // t
    return pl.pallas_call(
        _flash_kernel,
        grid=(bsz, nt, nt),
        in_specs=[pl.BlockSpec((1, HEADS, t, MLA_QK), lambda b, i, j: (b, 0, i, 0)),
                  pl.BlockSpec((1, HEADS, t, MLA_QK), lambda b, i, j: (b, 0, jnp.minimum(i, j), 0)),
                  pl.BlockSpec((1, HEADS, VT_ROWS, t), lambda b, i, j: (b, 0, 0, jnp.minimum(i, j))),
                  pl.BlockSpec((1, HEADS * HV), lambda b, i, j: (0, 0))],
        out_specs=pl.BlockSpec((t, HEADS * HV), lambda b, i, j: (b * nt + i, 0)),
        out_shape=jax.ShapeDtypeStruct((bsz * seq, HEADS * HV), BF16),
        scratch_shapes=[pltpu.VMEM((HEADS, V7X_SUBLANES, t), F32),
                        pltpu.VMEM((HEADS, VT_ROWS, t), F32)],
        compiler_params=_params(("parallel", "parallel", "arbitrary"), 56),
        name="mla_flash",
    )(q, k, vt, gain)


def _outproj_kernel(h_ref, a_ref, b_ref, c_ref, d_ref, w_ref, g_ref, o_ref):
    y = _dot(a_ref[...], w_ref[0:512, :])
    y += _dot(b_ref[...], w_ref[512:1024, :])
    y += _dot(c_ref[...], w_ref[1024:1536, :])
    y += _dot(d_ref[...], w_ref[1536:2048, :])
    o_ref[...] = h_ref[...] + _rms(y, g_ref[...])


def _outproj(h, mixes, w, gain, tm):
    n, d = h.shape
    row = lambda i: (i, 0)
    vmem = (4 * tm * d * 4 + d * d * 2 + 8 * tm * 512 * 2 + 2 * tm * d * 4) / MIB + 6
    return pl.pallas_call(
        _outproj_kernel,
        grid=(n // tm,),
        in_specs=[pl.BlockSpec((tm, d), row)] + [pl.BlockSpec((tm, 512), row)] * 4 +
                 [pl.BlockSpec((d, d), lambda i: (0, 0), pipeline_mode=pl.Buffered(1)),
                  pl.BlockSpec((1, d), lambda i: (0, 0))],
        out_specs=pl.BlockSpec((tm, d), row),
        out_shape=jax.ShapeDtypeStruct((n, d), F32),
        compiler_params=_params(("parallel",), vmem),
        name="outproj",
    )(h, *mixes, w, gain)


def _ple_kernel(h_ref, p_ref, g6_ref, g7_ref, wg_ref, wp_ref, o_ref):
    h = h_ref[...]
    gate = jax.nn.sigmoid(_dot(_rms(h, g6_ref[...]).astype(BF16), wg_ref[...]))
    emb = _rms(_dot(p_ref[...].astype(BF16), wp_ref[...]), g7_ref[...])
    o_ref[...] = h + gate * emb


def _ple(h, p, g6, g7, wg, wp, tm):
    n, d = h.shape
    row = lambda i: (i, 0)
    c2 = lambda i: (0, 0)
    vmem = (4 * tm * d * 4 + d * d * 2 + 2 * PLE_DIM * d * 2 + 3 * tm * d * 4) / MIB + 6
    return pl.pallas_call(
        _ple_kernel,
        grid=(n // tm,),
        in_specs=[pl.BlockSpec((tm, d), row), pl.BlockSpec((tm, PLE_DIM), row),
                  pl.BlockSpec((1, d), c2), pl.BlockSpec((1, d), c2),
                  pl.BlockSpec((d, d), c2, pipeline_mode=pl.Buffered(1)),
                  pl.BlockSpec((PLE_DIM, d), c2)],
        out_specs=pl.BlockSpec((tm, d), row),
        out_shape=jax.ShapeDtypeStruct((n, d), F32),
        compiler_params=_params(("parallel",), vmem),
        name="ple",
    )(h, p, g6, g7, wg, wp)


def _permute_w_in(w_in):
    sizes = (SSM_WIDTH, 256, 256, 512, 512, 4, 4, MLA_Q_RANK, MLA_KV_RANK, QK, 256, 256, 512, 512)
    offs = np.concatenate([[0], np.cumsum(sizes)])
    (s_u, m_q, m_k, m_v, m_o, m_i, m_f, a_cq, a_ckv, a_kr, r_q, r_k, r_v, r_g) = [
        w_in[:, int(offs[i]):int(offs[i + 1])] for i in range(len(sizes))]
    pad = jnp.zeros((w_in.shape[0], SLAB_WIDTH - QK - 2 * HEADS), w_in.dtype)
    main = jnp.concatenate([s_u, m_v, m_o, r_v, r_g, m_q, m_k, a_ckv, r_q, r_k, a_cq,
                            a_kr, m_i, m_f, pad], axis=1).astype(BF16)
    gates_t = jnp.concatenate([m_i, m_f], axis=1).T.astype(BF16)
    return main, gates_t


def _permute_mla(w_uq, w_ukv):
    uq = w_uq.reshape(MLA_Q_RANK, HEADS, MLA_QK)
    uq = jnp.concatenate([uq[:, :, :MLA_NOPE].reshape(MLA_Q_RANK, -1),
                          uq[:, :, MLA_NOPE:].reshape(MLA_Q_RANK, -1)], axis=1)
    ukv = w_ukv.reshape(MLA_KV_RANK, HEADS, MLA_NOPE + HV)
    uk = ukv[:, :, :MLA_NOPE].reshape(MLA_KV_RANK, -1)
    uvt = ukv[:, :, MLA_NOPE:].reshape(MLA_KV_RANK, -1).T
    return uq.astype(BF16), uk.astype(BF16), uvt.astype(BF16)


def _tile(n, want):
    t = min(n, want)
    assert n % t == 0, (n, t)
    return t


def kernel(x, p, positions, norm_gains, ffn_w_gate, ffn_w_up, ffn_w_down, w_in, w_out, mix_norm_gain, ssm_a_re, ssm_a_im, ssm_b_re, ssm_b_im, ssm_c_re, ssm_c_im, ssm_d, ssm_log_dt, ssm_w_glu, ssm_b_glu, mlstm_conv_w, mlstm_conv_b, mlstm_b_i, mlstm_b_f, mla_q_norm, mla_kv_norm, mla_w_uq, mla_w_ukv, ple_w_proj, ple_w_gate):
    bsz, seq, d = x.shape
    n = bsz * seq
    depth = p.shape[0]
    assert d == D_MODEL and seq % CHUNK == 0

    tm_ffn = _tile(n, 1024)
    tf_ffn = _tile(D_FF, 512)
    tm_proj = _tile(n, 256)
    tm_row = _tile(n, 512)
    t_seq = _tile(seq, 512)
    t_s5 = _tile(seq, 512)
    t_att = _tile(seq, 2048)

    rope = _rope_table(positions, _tile(n, 2048))
    h = x.reshape(n, d).astype(F32)
    gains = norm_gains.astype(F32).reshape(depth, 8, 1, d)
    mixg = mix_norm_gain.astype(F32).reshape(depth, 4, 1, 512)
    wg_all, wu_all, wd_all = ffn_w_gate.astype(BF16), ffn_w_up.astype(BF16), ffn_w_down.astype(BF16)

    for i in range(depth):
        g = gains[i]
        h = _ffn(h, g[0], g[1], wg_all[i, 0], wu_all[i, 0], wd_all[i, 0], tm_ffn, tf_ffn)

        w_main, w_gt = _permute_w_in(w_in[i])
        z, slab, gt = _inproj(h, g[2], w_main, w_gt, tm_proj)

        wb, wba, tabs, wc = _s5_tables(ssm_a_re[i], ssm_a_im[i], ssm_b_re[i], ssm_b_im[i],
                                       ssm_c_re[i], ssm_c_im[i], ssm_log_dt[i])
        out_a = _s5(z, wb, wba, tabs, wc, ssm_d[i].astype(F32).reshape(1, -1), ssm_w_glu[i].astype(BF16),
                    ssm_b_glu[i].astype(F32).reshape(1, -1), mixg[i, 0], bsz, seq, t_s5)
        out_b = _mlstm(z, slab, gt, mlstm_conv_w[i], mlstm_conv_b[i], mlstm_b_i[i], mlstm_b_f[i],
                       mixg[i, 1], bsz, seq, t_seq)
        wuq, wuk, wuvt = _permute_mla(mla_w_uq[i], mla_w_ukv[i])
        q, k, vt = _mla_proj(z, slab, rope, mla_q_norm[i].astype(F32).reshape(1, -1),
                             mla_kv_norm[i].astype(F32).reshape(1, -1), wuq, wuk, wuvt, bsz, seq, t_seq)
        out_c = _flash(q, k, vt, mixg[i, 2], t_att)
        out_d = _retention(z, rope, mixg[i, 3], bsz, seq, t_seq)

        h = _outproj(h, (out_a, out_b, out_c, out_d), w_out[i].astype(BF16), g[3], tm_row)
        h = _ffn(h, g[4], g[5], wg_all[i, 1], wu_all[i, 1], wd_all[i, 1], tm_ffn, tf_ffn)
        h = _ple(h, p[i].reshape(n, PLE_DIM), g[6], g[7], ple_w_gate[i].astype(BF16),
                 ple_w_proj[i].astype(BF16), tm_row)
    return h.reshape(bsz, seq, d).astype(x.dtype)
```

```python
import functools
import math

import jax
import jax.numpy as jnp
import numpy as np
from jax import lax
from jax.experimental import pallas as pl
from jax.experimental.pallas import tpu as pltpu

F32 = jnp.float32
BF16 = jnp.bfloat16

EPS = 1e-6
ROPE_BASE = 10000.0

D_MODEL = 2048
PLE_DIM = 256
D_FF = 5632

SSM_WIDTH = 512
SSM_GROUP = 16
SSM_GROUPS = SSM_WIDTH // SSM_GROUP
SSM_STATE = 64
SSM_LANES = SSM_GROUPS * SSM_STATE

HEADS = 4
QK = 64
HV = 128
CHUNK = 256
MLSTM_CONV = 4
MLA_Q_RANK = 384
MLA_KV_RANK = 256
MLA_NOPE = 128
MLA_QK = MLA_NOPE + QK

V7X_LANES = 128
V7X_SUBLANES = 8
V7X_VMEM_BYTES = 64 * 1024 * 1024
MIB = 1024 * 1024

Z_SU, Z_MV, Z_MO, Z_RV, Z_RG = 0, 512, 1024, 1536, 2048
Z_MQ, Z_MK, Z_CKV, Z_RQ, Z_RK = 2560, 2816, 3072, 3328, 3584
Z_CQ = 3840
Z_WIDTH = 4224
SLAB_WIDTH = 128
SLAB_I, SLAB_F = 64, 68


def _params(sem, vmem_mib):
    return pltpu.CompilerParams(dimension_semantics=sem, vmem_limit_bytes=int(vmem_mib * MIB))


def _rms(x, gain):
    ms = jnp.mean(x * x, axis=-1, keepdims=True)
    return x * lax.rsqrt(ms + EPS) * gain


def _dot(a, b):
    return jnp.dot(a, b, preferred_element_type=F32)


def _dot_nt(a, b):
    return lax.dot_general(a, b, (((1,), (1,)), ((), ())), preferred_element_type=F32)


def _dot_tn(a, b):
    return lax.dot_general(a, b, (((0,), (0,)), ((), ())), preferred_element_type=F32)


def _split_dot(tri, x, nt):
    hi = x.astype(BF16)
    lo = (x - hi.astype(F32)).astype(BF16)
    if nt:
        return _dot(hi, tri) + _dot(lo, tri)
    return _dot(tri, hi) + _dot(tri, lo)


def _rope_kernel(pos_ref, inv_ref, o_ref):
    ang = pos_ref[...].astype(F32) * inv_ref[...]
    lane = lax.broadcasted_iota(jnp.int32, ang.shape, 1)
    sgn = jnp.where(lane < 96, -1.0, 1.0).astype(F32)
    o_ref[...] = jnp.where(lane < 64, jnp.cos(ang), sgn * jnp.sin(ang))


def _rope_table(positions, tm):
    n = positions.size
    inv = 1.0 / (ROPE_BASE ** (jnp.arange(0, QK, 2, dtype=F32) / QK))
    inv = jnp.tile(inv, 4).reshape(1, 128)
    return pl.pallas_call(
        _rope_kernel,
        grid=(n // tm,),
        in_specs=[pl.BlockSpec((tm, 1), lambda i: (i, 0)),
                  pl.BlockSpec((1, 128), lambda i: (0, 0))],
        out_specs=pl.BlockSpec((tm, 128), lambda i: (i, 0)),
        out_shape=jax.ShapeDtypeStruct((n, 128), F32),
        compiler_params=_params(("parallel",), 32),
        name="rope_table",
    )(positions.reshape(n, 1), inv)


def _apply_rope(x, rope, nheads):
    width = nheads * QK
    cos = rope[:, :QK]
    sin = rope[:, QK:]
    if nheads > 1:
        cos = jnp.concatenate([cos] * nheads, axis=-1)
        sin = jnp.concatenate([sin] * nheads, axis=-1)
    if width >= V7X_LANES:
        lane = lax.broadcasted_iota(jnp.int32, x.shape, 1)
        first = (lane % QK) < (QK // 2)
        rot = jnp.where(first, pltpu.roll(x, width - QK // 2, 1), pltpu.roll(x, QK // 2, 1))
    else:
        rot = jnp.concatenate([x[:, QK // 2:], x[:, :QK // 2]], axis=-1)
    return x * cos + rot * sin


def _ffn_kernel(h_ref, gpre_ref, gpost_ref, wg_ref, wu_ref, wd_ref, o_ref, xn_ref):
    j = pl.program_id(1)

    def swiglu_tile(x):
        g = _dot(x, wg_ref[...])
        u = _dot(x, wu_ref[...])
        a = (g * jax.nn.sigmoid(g) * u).astype(BF16)
        return _dot(a, wd_ref[...])

    @pl.when(j == 0)
    def _():
        x = _rms(h_ref[...], gpre_ref[...]).astype(BF16)
        xn_ref[...] = x
        o_ref[...] = swiglu_tile(x)

    last = pl.num_programs(1) - 1

    @pl.when((j > 0) & (j < last))
    def _():
        o_ref[...] += swiglu_tile(xn_ref[...])

    @pl.when(j == last)
    def _():
        y = o_ref[...] + swiglu_tile(xn_ref[...])
        o_ref[...] = h_ref[...] + _rms(y, gpost_ref[...])


def _ffn(h, gpre, gpost, wg, wu, wd, tm, tf):
    n, d = h.shape
    f = wg.shape[1]
    vmem = (4 * tm * d * 4 + tm * d * 2 + 6 * d * tf * 2 + 6 * tm * tf * 4) / MIB + 4
    return pl.pallas_call(
        _ffn_kernel,
        grid=(n // tm, f // tf),
        in_specs=[pl.BlockSpec((tm, d), lambda i, j: (i, 0), pipeline_mode=pl.Buffered(1)),
                  pl.BlockSpec((1, d), lambda i, j: (0, 0)),
                  pl.BlockSpec((1, d), lambda i, j: (0, 0)),
                  pl.BlockSpec((d, tf), lambda i, j: (0, j)),
                  pl.BlockSpec((d, tf), lambda i, j: (0, j)),
                  pl.BlockSpec((tf, d), lambda i, j: (j, 0))],
        out_specs=pl.BlockSpec((tm, d), lambda i, j: (i, 0)),
        out_shape=jax.ShapeDtypeStruct((n, d), F32),
        scratch_shapes=[pltpu.VMEM((tm, d), BF16)],
        compiler_params=_params(("parallel", "arbitrary"), vmem),
        name="ffn",
    )(h, gpre, 0.5 * gpost, wg, wu, wd)


def _inproj_kernel(h_ref, g_ref, w_ref, wgt_ref, z_ref, slab_ref, gt_ref):
    xn = _rms(h_ref[...], g_ref[...]).astype(BF16)
    z = _dot(xn, w_ref[...])
    z_ref[...] = z[:, :Z_WIDTH].astype(BF16)
    slab_ref[...] = z[:, Z_WIDTH:]
    gt_ref[...] = _dot_nt(wgt_ref[...], xn)


def _inproj(h, gain, w, wgt, tm):
    n, d = h.shape
    wtot = Z_WIDTH + SLAB_WIDTH
    vmem = (2 * tm * d * 4 + d * wtot * 2 + 2 * tm * wtot * 4 + tm * wtot * 4) / MIB + 6
    return pl.pallas_call(
        _inproj_kernel,
        grid=(n // tm,),
        in_specs=[pl.BlockSpec((tm, d), lambda i: (i, 0)),
                  pl.BlockSpec((1, d), lambda i: (0, 0)),
                  pl.BlockSpec((d, wtot), lambda i: (0, 0), pipeline_mode=pl.Buffered(1)),
                  pl.BlockSpec((8, d), lambda i: (0, 0))],
        out_specs=[pl.BlockSpec((tm, Z_WIDTH), lambda i: (i, 0)),
                   pl.BlockSpec((tm, SLAB_WIDTH), lambda i: (i, 0)),
                   pl.BlockSpec((8, tm), lambda i: (0, i))],
        out_shape=[jax.ShapeDtypeStruct((n, Z_WIDTH), BF16),
                   jax.ShapeDtypeStruct((n, SLAB_WIDTH), F32),
                   jax.ShapeDtypeStruct((8, n), F32)],
        compiler_params=_params(("parallel",), vmem),
        name="inproj",
    )(h, gain, w, wgt)


SCAN_COLS = 512
S5_HALVES = 2
S5_SUB = 256


def _s5_kernel(u_ref, wb_ref, wba_ref, tab_ref, wc_ref, d_ref, wglu_ref, bglu_ref, gain_ref,
               o_ref, x_ref, carry_ref):
    t = u_ref.shape[0]
    hl = SSM_LANES // S5_HALVES
    hw = SSM_WIDTH // S5_HALVES

    @pl.when(pl.program_id(1) == 0)
    def _():
        carry_ref[...] = jnp.zeros_like(carry_ref)

    lanes = []
    for c0 in range(0, SSM_LANES, SCAN_COLS):
        k, off = divmod(c0, hl)
        lanes.append((slice(2 * hl * k + off, 2 * hl * k + off + SCAN_COLS),
                      slice(2 * hl * k + hl + off, 2 * hl * k + hl + off + SCAN_COLS),
                      slice(c0, c0 + SCAN_COLS)))
    last = V7X_SUBLANES - 1

    def project_in(rows):
        u = u_ref[rows, :]
        uf = u.astype(F32)
        first = lax.broadcasted_iota(jnp.int32, uf.shape, 0) % V7X_SUBLANES == 0
        prev = jnp.where(first, 0.0, pltpu.roll(uf, 1, 0)).astype(BF16)
        for k in range(S5_HALVES):
            cols = slice(hw * k, hw * (k + 1))
            x_ref[rows, 2 * hl * k:2 * hl * (k + 1)] = _dot(u[:, cols], wb_ref[k]) + _dot(prev[:, cols], wba_ref[k])

    def scan(row0, nrows):
        for re, im, tab in lanes:
            cr, ci = carry_ref[:, re], carry_ref[:, im]
            for r0 in range(row0, row0 + nrows, V7X_SUBLANES):
                rows = slice(r0, r0 + V7X_SUBLANES)
                xr = x_ref[rows, re]
                xi = x_ref[rows, im]
                for j, shift in ((1, 2), (2, 4)):
                    ar, ai = tab_ref[2 * j, :, tab], tab_ref[2 * j + 1, :, tab]
                    rr = pltpu.roll(xr, shift, 0)
                    ri = pltpu.roll(xi, shift, 0)
                    xr, xi = xr + ar * rr - ai * ri, xi + ar * ri + ai * rr
                pr, pi = tab_ref[6, :, tab], tab_ref[7, :, tab]
                xr, xi = xr + pr * cr - pi * ci, xi + pr * ci + pi * cr
                x_ref[rows, re] = xr
                x_ref[rows, im] = xi
                cr, ci = jnp.broadcast_to(xr[last:, :], xr.shape), jnp.broadcast_to(xi[last:, :], xi.shape)
            carry_ref[:, re] = cr
            carry_ref[:, im] = ci

    def project_out(rows):
        y = jnp.concatenate([_dot(x_ref[rows, 2 * hl * k:2 * hl * (k + 1)].astype(BF16), wc_ref[k])
                             for k in range(S5_HALVES)], axis=-1)
        y = y + d_ref[...] * u_ref[rows, :].astype(F32)
        g = jax.nn.gelu(y)
        out = g * jax.nn.sigmoid(_dot(g.astype(BF16), wglu_ref[...]) + bglu_ref[...])
        o_ref[rows, :] = _rms(out, gain_ref[...]).astype(o_ref.dtype)

    sub = min(S5_SUB, t)
    for r0 in range(0, t, sub):
        project_in(slice(r0, r0 + sub))
    for r0 in range(0, t, sub):
        scan(r0, sub)
        project_out(slice(r0, r0 + sub))


def _s5(z, wb, wba, tabs, wc, d_skip, wglu, bglu, gain, bsz, seq, t):
    n = bsz * seq
    nt = seq // t
    nl2 = 2 * SSM_LANES
    const = lambda b, i: (0, 0)
    vmem = (t * nl2 * 4 * 3 + 2 * SSM_WIDTH * nl2 * 2 * 2 + 8 * 8 * SSM_LANES * 4 * 2) / MIB + 8
    return pl.pallas_call(
        _s5_kernel,
        grid=(bsz, nt),
        in_specs=[pl.BlockSpec((t, SSM_WIDTH), lambda b, i: (b * nt + i, Z_SU // SSM_WIDTH)),
                  pl.BlockSpec((S5_HALVES, SSM_WIDTH // S5_HALVES, nl2 // S5_HALVES), lambda b, i: (0, 0, 0)),
                  pl.BlockSpec((S5_HALVES, SSM_WIDTH // S5_HALVES, nl2 // S5_HALVES), lambda b, i: (0, 0, 0)),
                  pl.BlockSpec((8, V7X_SUBLANES, SSM_LANES), lambda b, i: (0, 0, 0)),
                  pl.BlockSpec((S5_HALVES, nl2 // S5_HALVES, SSM_WIDTH // S5_HALVES), lambda b, i: (0, 0, 0)),
                  pl.BlockSpec((1, SSM_WIDTH), const),
                  pl.BlockSpec((SSM_WIDTH, SSM_WIDTH), const),
                  pl.BlockSpec((1, SSM_WIDTH), const),
                  pl.BlockSpec((1, SSM_WIDTH), const)],
        out_specs=pl.BlockSpec((t, SSM_WIDTH), lambda b, i: (b * nt + i, 0)),
        out_shape=jax.ShapeDtypeStruct((n, SSM_WIDTH), BF16),
        scratch_shapes=[pltpu.VMEM((t, nl2), F32), pltpu.VMEM((V7X_SUBLANES, nl2), F32)],
        compiler_params=_params(("arbitrary", "arbitrary"), vmem),
        name="s5",
    )(z, wb, wba, tabs, wc, d_skip, wglu, bglu, gain)


def _s5_tables(a_re, a_im, b_re, b_im, c_re, c_im, log_dt):
    lr, li = a_re.astype(F32), a_im.astype(F32)
    dt = jnp.exp(log_dt.astype(F32))[:, None]
    mag = jnp.exp(lr * dt)
    ar, ai = mag * jnp.cos(li * dt), mag * jnp.sin(li * dt)
    den = lr * lr + li * li
    nr, ni = ar - 1.0, ai
    zr = (nr * lr + ni * li) / den
    zi = (ni * lr - nr * li) / den
    br_, bi_ = b_re.astype(F32), b_im.astype(F32)
    bbr = zr[..., None] * br_ - zi[..., None] * bi_
    bbi = zr[..., None] * bi_ + zi[..., None] * br_
    eye = jnp.eye(SSM_GROUPS, dtype=F32)
    wb_re = jnp.einsum('gnc,gh->gchn', bbr, eye).reshape(SSM_WIDTH, SSM_LANES)
    wb_im = jnp.einsum('gnc,gh->gchn', bbi, eye).reshape(SSM_WIDTH, SSM_LANES)
    hw, hl = SSM_WIDTH // S5_HALVES, SSM_LANES // S5_HALVES

    def halves_in(w_re, w_im):
        return jnp.stack([jnp.concatenate([w[hw * k:hw * (k + 1), hl * k:hl * (k + 1)] for w in (w_re, w_im)], axis=1)
                          for k in range(S5_HALVES)]).astype(BF16)

    ar_l, ai_l = ar.reshape(1, SSM_LANES), ai.reshape(1, SSM_LANES)
    wb = halves_in(wb_re, wb_im)
    wba = halves_in(wb_re * ar_l - wb_im * ai_l, wb_re * ai_l + wb_im * ar_l)
    wc_re = jnp.einsum('gcn,gh->gnhc', c_re.astype(F32), eye).reshape(SSM_LANES, SSM_WIDTH)
    wc_im = jnp.einsum('gcn,gh->gnhc', c_im.astype(F32), eye).reshape(SSM_LANES, SSM_WIDTH)
    wc = jnp.stack([jnp.concatenate([w[hl * k:hl * (k + 1), hw * k:hw * (k + 1)] for w in (wc_re, -wc_im)], axis=0)
                    for k in range(S5_HALVES)]).astype(BF16)

    a1 = (ar.reshape(1, SSM_LANES), ai.reshape(1, SSM_LANES))

    def cmul(p, q):
        return (p[0] * q[0] - p[1] * q[1], p[0] * q[1] + p[1] * q[0])

    pows = [a1]
    for _ in range(7):
        pows.append(cmul(pows[-1], a1))
    row = jnp.arange(V7X_SUBLANES)[:, None]
    tabs = []
    for shift in (1, 2, 4):
        pr, pi = pows[shift - 1]
        keep = row >= shift
        tabs += [jnp.where(keep, pr, 0.0), jnp.where(keep, pi, 0.0)]
    tabs += [jnp.concatenate([p[0] for p in pows], axis=0), jnp.concatenate([p[1] for p in pows], axis=0)]
    return wb, wba, jnp.stack(tabs).astype(F32), wc


def _ret_kernel(q_ref, k_ref, v_ref, g_ref, rope_ref, dmask_ref, qdec_ref, kdec_ref, cdec_ref, gain_ref,
                o_ref, state_ref):
    t = q_ref.shape[0]

    @pl.when(pl.program_id(1) == 0)
    def _():
        state_ref[...] = jnp.zeros_like(state_ref)

    for c in range(t // CHUNK):
        rows = slice(c * CHUNK, (c + 1) * CHUNK)
        rope = rope_ref[rows, :]
        q = _apply_rope(q_ref[rows, :].astype(F32), rope, HEADS)
        k = _apply_rope(k_ref[rows, :].astype(F32), rope, HEADS) * (QK ** -0.5)
        for h in range(HEADS):
            qh = q[:, h * QK:(h + 1) * QK].astype(BF16)
            kh = k[:, h * QK:(h + 1) * QK]
            vh = v_ref[rows, h * HV:(h + 1) * HV]
            s = _dot_nt(qh, kh.astype(BF16)) * dmask_ref[h]
            inner = _dot(s.astype(BF16), vh)
            r_old = state_ref[h]
            cross = _dot(qh, r_old.astype(BF16)) * qdec_ref[h]
            state_ref[h] = cdec_ref[h] * r_old + _dot_tn((kh * kdec_ref[h]).astype(BF16), vh)
            y = _rms(inner + cross, gain_ref[:, h * HV:(h + 1) * HV])
            gate = g_ref[rows, h * HV:(h + 1) * HV].astype(F32)
            o_ref[rows, h * HV:(h + 1) * HV] = (gate * jax.nn.sigmoid(gate) * y).astype(o_ref.dtype)


def _ret_tables():
    log_g = jnp.log1p(-jnp.exp2(-5.0 - jnp.arange(HEADS, dtype=F32)))
    idx = jnp.arange(CHUNK)
    diff = idx[:, None] - idx[None, :]
    causal = diff >= 0
    dmask = jnp.where(causal, jnp.exp(jnp.where(causal, diff, 0).astype(F32) * log_g[:, None, None]), 0.0)
    q_decay = jnp.exp((idx + 1).astype(F32)[None, :] * log_g[:, None])
    k_decay = jnp.exp((CHUNK - 1 - idx).astype(F32)[None, :] * log_g[:, None])
    chunk_decay = jnp.exp(CHUNK * log_g)
    qdec = jnp.broadcast_to(q_decay[:, :, None], (HEADS, CHUNK, HV))
    kdec = jnp.broadcast_to(k_decay[:, :, None], (HEADS, CHUNK, QK))
    cdec = jnp.broadcast_to(chunk_decay[:, None, None], (HEADS, QK, HV))
    return dmask, qdec, kdec, cdec


def _retention(z, rope, gain, bsz, seq, t):
    n = bsz * seq
    nt = seq // t
    dmask, qdec, kdec, cdec = _ret_tables()
    c3 = lambda b, i: (0, 0, 0)

    def col(width, lane0):
        return pl.BlockSpec((t, width), lambda b, i: (b * nt + i, lane0 // width))

    return pl.pallas_call(
        _ret_kernel,
        grid=(bsz, nt),
        in_specs=[col(256, Z_RQ), col(256, Z_RK), col(512, Z_RV), col(512, Z_RG),
                  pl.BlockSpec((t, 128), lambda b, i: (b * nt + i, 0)),
                  pl.BlockSpec((HEADS, CHUNK, CHUNK), c3),
                  pl.BlockSpec((HEADS, CHUNK, HV), c3),
                  pl.BlockSpec((HEADS, CHUNK, QK), c3),
                  pl.BlockSpec((HEADS, QK, HV), c3),
                  pl.BlockSpec((1, HEADS * HV), lambda b, i: (0, 0))],
        out_specs=pl.BlockSpec((t, HEADS * HV), lambda b, i: (b * nt + i, 0)),
        out_shape=jax.ShapeDtypeStruct((n, HEADS * HV), BF16),
        scratch_shapes=[pltpu.VMEM((HEADS, QK, HV), F32)],
        compiler_params=_params(("arbitrary", "arbitrary"), 32),
        name="retention",
    )(z, z, z, z, rope, dmask, qdec, kdec, cdec, gain)


TAIL = V7X_SUBLANES


def _mlstm_kernel(q_ref, k_ref, v_ref, og_ref, slab_ref, gt_ref, cw_ref, cb_ref, brow_ref, bcol_ref,
                  tril_ref, triu_ref, gain_ref, o_ref, xbuf_ref, c_ref, m_ref):
    t = q_ref.shape[0]
    width = 2 * HEADS * QK

    @pl.when(pl.program_id(1) == 0)
    def _():
        xbuf_ref[0:TAIL, :] = jnp.zeros((TAIL, width), F32)
        c_ref[...] = jnp.zeros_like(c_ref)
        m_ref[...] = jnp.zeros_like(m_ref)

    xbuf_ref[TAIL:, 0:HEADS * QK] = q_ref[...].astype(F32)
    xbuf_ref[TAIL:, HEADS * QK:] = k_ref[...].astype(F32)
    acc = cb_ref[...] + cw_ref[MLSTM_CONV - 1:MLSTM_CONV, :] * xbuf_ref[TAIL:, :]
    for j in range(MLSTM_CONV - 1):
        off = TAIL - (MLSTM_CONV - 1) + j
        acc = acc + cw_ref[j:j + 1, :] * xbuf_ref[off:off + t, :]
    xbuf_ref[0:TAIL, :] = xbuf_ref[t:t + TAIL, :]
    qk = acc * jax.nn.sigmoid(acc)

    gr = gt_ref[...] + brow_ref[...]
    is_f_row = lax.broadcasted_iota(jnp.int32, gr.shape, 0) >= HEADS
    lf_r = jnp.where(is_f_row, jax.nn.log_sigmoid(gr), 0.0)
    gc = slab_ref[...] + bcol_ref[...]
    lane = lax.broadcasted_iota(jnp.int32, gc.shape, 1)
    is_f_col = (lane >= SLAB_F) & (lane < SLAB_F + HEADS)
    lf_c = jnp.where(is_f_col, jax.nn.log_sigmoid(gc), 0.0)

    row = lax.broadcasted_iota(jnp.int32, (CHUNK, CHUNK), 0)
    colid = lax.broadcasted_iota(jnp.int32, (CHUNK, CHUNK), 1)
    causal = colid <= row
    ones_col = (lax.broadcasted_iota(jnp.int32, (CHUNK, HV), 1) == 0).astype(BF16)

    for c in range(t // CHUNK):
        rows = slice(c * CHUNK, (c + 1) * CHUNK)
        bh_r = _split_dot(triu_ref[...], lf_r[:, rows], nt=True)
        bh_c = _split_dot(tril_ref[...], lf_c[rows, :], nt=False)
        a_r = gr[0:HEADS, rows] - bh_r[HEADS:, :]
        a_c = gc[rows, SLAB_I:SLAB_I + HEADS] - bh_c[:, SLAB_F:SLAB_F + HEADS]
        bh_c = bh_c[:, SLAB_F:SLAB_F + HEADS]
        hs = range(HEADS)
        qh = [(qk[rows, h * QK:(h + 1) * QK] * (QK ** -0.5)).astype(BF16) for h in hs]
        kh = [qk[rows, (HEADS + h) * QK:(HEADS + h + 1) * QK] for h in hs]
        vaug = [jnp.concatenate([v_ref[rows, h * HV:(h + 1) * HV], ones_col], axis=-1) for h in hs]
        m_old = [m_ref[h][0:1, 0:1] for h in hs]
        c_old = [c_ref[h] for h in hs]
        qk_raw = [_dot_nt(qh[h], kh[h].astype(BF16)) for h in hs]
        q_c = [_dot(qh[h], c_old[h].astype(BF16)) for h in hs]
        amat = [jnp.where(causal, a_r[h:h + 1, :], -jnp.inf) for h in hs]
        big_m = [jnp.maximum(jnp.max(amat[h], axis=-1, keepdims=True), m_old[h]) for h in hs]
        s = [(qk_raw[h] * jnp.exp(amat[h] - big_m[h])).astype(BF16) for h in hs]
        sc = [jnp.exp(m_old[h] - big_m[h]) for h in hs]
        tot = [_dot(s[h], vaug[h]) + sc[h] * q_c[h] for h in hs]
        m_last = [big_m[h][CHUNK - 1:CHUNK, :] for h in hs]
        wk = [jnp.exp(a_c[:, h:h + 1] - m_last[h]) for h in hs]
        upd = [_dot_tn((kh[h] * wk[h]).astype(BF16), vaug[h]) for h in hs]
        for h in hs:
            c_ref[h] = jnp.exp(m_old[h] - m_last[h]) * c_old[h] + upd[h]
            m_ref[h] = jnp.broadcast_to(bh_c[CHUNK - 1:CHUNK, h:h + 1] + m_last[h], m_ref.shape[1:])
        for h in hs:
            den = jnp.maximum(jnp.abs(tot[h][:, HV:HV + 1]), jnp.exp(-(bh_c[:, h:h + 1] + big_m[h])))
            y = _rms(tot[h][:, :HV] / den, gain_ref[:, h * HV:(h + 1) * HV])
            og = og_ref[rows, h * HV:(h + 1) * HV].astype(F32)
            o_ref[rows, h * HV:(h + 1) * HV] = (jax.nn.sigmoid(og) * y).astype(o_ref.dtype)


def _mlstm(z, slab, gt, conv_w, conv_b, b_i, b_f, gain, bsz, seq, t):
    n = bsz * seq
    nt = seq // t
    idx = jnp.arange(CHUNK)
    tril = (idx[None, :] <= idx[:, None]).astype(BF16)
    triu = tril.T
    brow = jnp.concatenate([b_i, b_f]).astype(F32).reshape(2 * HEADS, 1)
    bcol = jnp.zeros((1, SLAB_WIDTH), F32).at[0, SLAB_I:SLAB_I + 2 * HEADS].set(brow[:, 0])
    c2 = lambda b, i: (0, 0)

    def col(width, lane0):
        return pl.BlockSpec((t, width), lambda b, i: (b * nt + i, lane0 // width))

    return pl.pallas_call(
        _mlstm_kernel,
        grid=(bsz, nt),
        in_specs=[col(256, Z_MQ), col(256, Z_MK), col(512, Z_MV), col(512, Z_MO),
                  pl.BlockSpec((t, SLAB_WIDTH), lambda b, i: (b * nt + i, 0)),
                  pl.BlockSpec((2 * HEADS, t), lambda b, i: (0, b * nt + i)),
                  pl.BlockSpec((MLSTM_CONV, 2 * HEADS * QK), c2),
                  pl.BlockSpec((1, 2 * HEADS * QK), c2),
                  pl.BlockSpec((2 * HEADS, 1), c2),
                  pl.BlockSpec((1, SLAB_WIDTH), c2),
                  pl.BlockSpec((CHUNK, CHUNK), c2),
                  pl.BlockSpec((CHUNK, CHUNK), c2),
                  pl.BlockSpec((1, HEADS * HV), c2)],
        out_specs=pl.BlockSpec((t, HEADS * HV), lambda b, i: (b * nt + i, 0)),
        out_shape=jax.ShapeDtypeStruct((n, HEADS * HV), BF16),
        scratch_shapes=[pltpu.VMEM((t + TAIL, 2 * HEADS * QK), F32),
                        pltpu.VMEM((HEADS, QK, 2 * HV), F32),
                        pltpu.VMEM((HEADS, V7X_SUBLANES, V7X_LANES), F32)],
        compiler_params=_params(("arbitrary", "arbitrary"), 32),
        name="mlstm",
    )(z, z, z, z, slab, gt, conv_w.astype(F32), conv_b.astype(F32).reshape(1, -1), brow, bcol, tril, triu, gain)


VT_ROWS = HV + 16


def _mla_proj_kernel(cq_ref, ckv_ref, slab_ref, rope_ref, qn_ref, kvn_ref, wuq_ref, wuk_ref, wuvt_ref,
                     q_ref, k_ref, vt_ref):
    rope = rope_ref[...]
    cq = _rms(cq_ref[...].astype(F32), qn_ref[...]).astype(BF16)
    ckv = _rms(ckv_ref[...].astype(F32), kvn_ref[...]).astype(BF16)
    q = _dot(cq, wuq_ref[...])
    kn = _dot(ckv, wuk_ref[...])
    vt = _dot_nt(wuvt_ref[...], ckv)
    scale = (MLA_QK ** -0.5) * math.log2(math.e)
    q_rope = _apply_rope(q[:, HEADS * MLA_NOPE:], rope, HEADS)
    k_rope = _apply_rope(slab_ref[:, 0:QK], rope, 1)
    pad = VT_ROWS - HV
    ones_row = (lax.broadcasted_iota(jnp.int32, (pad, vt.shape[1]), 0) == 0).astype(BF16)
    for h in range(HEADS):
        q_ref[0, h, :, 0:MLA_NOPE] = (q[:, h * MLA_NOPE:(h + 1) * MLA_NOPE] * scale).astype(BF16)
        q_ref[0, h, :, MLA_NOPE:] = (q_rope[:, h * QK:(h + 1) * QK] * scale).astype(BF16)
        k_ref[0, h, :, 0:MLA_NOPE] = kn[:, h * MLA_NOPE:(h + 1) * MLA_NOPE].astype(BF16)
        k_ref[0, h, :, MLA_NOPE:] = k_rope.astype(BF16)
        vt_ref[0, h, 0:HV, :] = vt[h * HV:(h + 1) * HV, :].astype(BF16)
        vt_ref[0, h, HV:, :] = ones_row


def _mla_proj(z, slab, rope, q_norm, kv_norm, wuq, wuk, wuvt, bsz, seq, t):
    nt = seq // t
    c2 = lambda b, i: (0, 0)
    return pl.pallas_call(
        _mla_proj_kernel,
        grid=(bsz, nt),
        in_specs=[pl.BlockSpec((t, MLA_Q_RANK), lambda b, i: (b * nt + i, Z_CQ // MLA_Q_RANK)),
                  pl.BlockSpec((t, MLA_KV_RANK), lambda b, i: (b * nt + i, Z_CKV // MLA_KV_RANK)),
                  pl.BlockSpec((t, SLAB_WIDTH), lambda b, i: (b * nt + i, 0)),
                  pl.BlockSpec((t, 128), lambda b, i: (b * nt + i, 0)),
                  pl.BlockSpec((1, MLA_Q_RANK), c2),
                  pl.BlockSpec((1, MLA_KV_RANK), c2),
                  pl.BlockSpec((MLA_Q_RANK, HEADS * MLA_QK), c2),
                  pl.BlockSpec((MLA_KV_RANK, HEADS * MLA_NOPE), c2),
                  pl.BlockSpec((HEADS * HV, MLA_KV_RANK), c2)],
        out_specs=[pl.BlockSpec((1, HEADS, t, MLA_QK), lambda b, i: (b, 0, i, 0)),
                   pl.BlockSpec((1, HEADS, t, MLA_QK), lambda b, i: (b, 0, i, 0)),
                   pl.BlockSpec((1, HEADS, VT_ROWS, t), lambda b, i: (b, 0, 0, i))],
        out_shape=[jax.ShapeDtypeStruct((bsz, HEADS, seq, MLA_QK), BF16),
                   jax.ShapeDtypeStruct((bsz, HEADS, seq, MLA_QK), BF16),
                   jax.ShapeDtypeStruct((bsz, HEADS, VT_ROWS, seq), BF16)],
        compiler_params=_params(("parallel", "parallel"), 40),
        name="mla_proj",
    )(z, z, slab, rope, q_norm, kv_norm, wuq, wuk, wuvt)


FLASH_SUB = 1024


def _flash_kernel(q_ref, k_ref, vt_ref, gain_ref, o_ref, m_ref, acc_ref):
    qi = pl.program_id(1)
    ki = pl.program_id(2)
    tk = k_ref.shape[2]
    sub = min(FLASH_SUB, tk)

    @pl.when(ki == 0)
    def _():
        m_ref[...] = jnp.full_like(m_ref, -jnp.inf)
        acc_ref[...] = jnp.zeros_like(acc_ref)

    def step(diagonal):
        chains = [(h, c) for h in range(HEADS) for c in range(tk // sub)]

        def scores(h, c):
            c0 = c * sub if diagonal else 0
            st = _dot_nt(k_ref[0, h, c * sub:(c + 1) * sub, :], q_ref[0, h, c0:, :])
            if diagonal:
                key = c * sub + lax.broadcasted_iota(jnp.int32, st.shape, 0)
                qry = c0 + lax.broadcasted_iota(jnp.int32, st.shape, 1)
                st = jnp.where(key <= qry, st, -jnp.inf)
            return st

        st_next = scores(*chains[0])
        for idx, (h, c) in enumerate(chains):
            st = st_next
            if idx + 1 < len(chains):
                st_next = scores(*chains[idx + 1])
            c0 = c * sub if diagonal else 0
            m_old = m_ref[h, 0:1, c0:]
            m_new = jnp.maximum(m_old, jnp.max(st, axis=0, keepdims=True))
            alpha = jnp.exp2(m_old - m_new)
            pt = jnp.exp2(st - m_new).astype(BF16)
            acc_ref[h, :, c0:] = alpha * acc_ref[h, :, c0:] + _dot(vt_ref[0, h, :, c * sub:(c + 1) * sub], pt)
            m_ref[h, :, c0:] = jnp.broadcast_to(m_new, (V7X_SUBLANES, m_new.shape[1]))

    @pl.when(ki < qi)
    def _():
        step(False)

    @pl.when(ki == qi)
    def _():
        step(True)
        for h in range(HEADS):
            acc = acc_ref[h]
            yt = acc[0:HV, :] / acc[HV:HV + 1, :]
            yt = yt * lax.rsqrt(jnp.mean(yt * yt, axis=0, keepdims=True) + EPS)
            o_ref[:, h * HV:(h + 1) * HV] = (yt.T * gain_ref[:, h * HV:(h + 1) * HV]).astype(o_ref.dtype)


def _flash(q, k, vt, gain, t):
    bsz, _, seq, _ = q.shape
    nt = seq // t
    return pl.pallas_call(
        _flash_kernel,
        grid=(bsz, nt, nt),
        in_specs=[pl.BlockSpec((1, HEADS, t, MLA_QK), lambda b, i, j: (b, 0, i, 0)),
                  pl.BlockSpec((1, HEADS, t, MLA_QK), lambda b, i, j: (b, 0, jnp.minimum(i, j), 0)),
                  pl.BlockSpec((1, HEADS, VT_ROWS, t), lambda b, i, j: (b, 0, 0, jnp.minimum(i, j))),
                  pl.BlockSpec((1, HEADS * HV), lambda b, i, j: (0, 0))],
        out_specs=pl.BlockSpec((t, HEADS * HV), lambda b, i, j: (b * nt + i, 0)),
        out_shape=jax.ShapeDtypeStruct((bsz * seq, HEADS * HV), BF16),
        scratch_shapes=[pltpu.VMEM((HEADS, V7X_SUBLANES, t), F32),
                        pltpu.VMEM((HEADS, VT_ROWS, t), F32)],
        compiler_params=_params(("parallel", "parallel", "arbitrary"), 56),
        name="mla_flash",
    )(q, k, vt, gain)


def _outproj_kernel(h_ref, a_ref, b_ref, c_ref, d_ref, w_ref, g_ref, o_ref):
    y = _dot(a_ref[...], w_ref[0:512, :])
    y += _dot(b_ref[...], w_ref[512:1024, :])
    y += _dot(c_ref[...], w_ref[1024:1536, :])
    y += _dot(d_ref[...], w_ref[1536:2048, :])
    o_ref[...] = h_ref[...] + _rms(y, g_ref[...])


def _outproj(h, mixes, w, gain, tm):
    n, d = h.shape
    row = lambda i: (i, 0)
    vmem = (4 * tm * d * 4 + d * d * 2 + 8 * tm * 512 * 2 + 2 * tm * d * 4) / MIB + 6
    return pl.pallas_call(
        _outproj_kernel,
        grid=(n // tm,),
        in_specs=[pl.BlockSpec((tm, d), row)] + [pl.BlockSpec((tm, 512), row)] * 4 +
                 [pl.BlockSpec((d, d), lambda i: (0, 0), pipeline_mode=pl.Buffered(1)),
                  pl.BlockSpec((1, d), lambda i: (0, 0))],
        out_specs=pl.BlockSpec((tm, d), row),
        out_shape=jax.ShapeDtypeStruct((n, d), F32),
        compiler_params=_params(("parallel",), vmem),
        name="outproj",
    )(h, *mixes, w, gain)


def _ple_kernel(h_ref, p_ref, g6_ref, g7_ref, wg_ref, wp_ref, o_ref):
    h = h_ref[...]
    gate = jax.nn.sigmoid(_dot(_rms(h, g6_ref[...]).astype(BF16), wg_ref[...]))
    emb = _rms(_dot(p_ref[...].astype(BF16), wp_ref[...]), g7_ref[...])
    o_ref[...] = h + gate * emb


def _ple(h, p, g6, g7, wg, wp, tm):
    n, d = h.shape
    row = lambda i: (i, 0)
    c2 = lambda i: (0, 0)
    vmem = (4 * tm * d * 4 + d * d * 2 + 2 * PLE_DIM * d * 2 + 3 * tm * d * 4) / MIB + 6
    return pl.pallas_call(
        _ple_kernel,
        grid=(n // tm,),
        in_specs=[pl.BlockSpec((tm, d), row), pl.BlockSpec((tm, PLE_DIM), row),
                  pl.BlockSpec((1, d), c2), pl.BlockSpec((1, d), c2),
                  pl.BlockSpec((d, d), c2, pipeline_mode=pl.Buffered(1)),
                  pl.BlockSpec((PLE_DIM, d), c2)],
        out_specs=pl.BlockSpec((tm, d), row),
        out_shape=jax.ShapeDtypeStruct((n, d), F32),
        compiler_params=_params(("parallel",), vmem),
        name="ple",
    )(h, p, g6, g7, wg, wp)


def _permute_w_in(w_in):
    sizes = (SSM_WIDTH, 256, 256, 512, 512, 4, 4, MLA_Q_RANK, MLA_KV_RANK, QK, 256, 256, 512, 512)
    offs = np.concatenate([[0], np.cumsum(sizes)])
    (s_u, m_q, m_k, m_v, m_o, m_i, m_f, a_cq, a_ckv, a_kr, r_q, r_k, r_v, r_g) = [
        w_in[:, int(offs[i]):int(offs[i + 1])] for i in range(len(sizes))]
    pad = jnp.zeros((w_in.shape[0], SLAB_WIDTH - QK - 2 * HEADS), w_in.dtype)
    main = jnp.concatenate([s_u, m_v, m_o, r_v, r_g, m_q, m_k, a_ckv, r_q, r_k, a_cq,
                            a_kr, m_i, m_f, pad], axis=1).astype(BF16)
    gates_t = jnp.concatenate([m_i, m_f], axis=1).T.astype(BF16)
    return main, gates_t


def _permute_mla(w_uq, w_ukv):
    uq = w_uq.reshape(MLA_Q_RANK, HEADS, MLA_QK)
    uq = jnp.concatenate([uq[:, :, :MLA_NOPE].reshape(MLA_Q_RANK, -1),
                          uq[:, :, MLA_NOPE:].reshape(MLA_Q_RANK, -1)], axis=1)
    ukv = w_ukv.reshape(MLA_KV_RANK, HEADS, MLA_NOPE + HV)
    uk = ukv[:, :, :MLA_NOPE].reshape(MLA_KV_RANK, -1)
    uvt = ukv[:, :, MLA_NOPE:].reshape(MLA_KV_RANK, -1).T
    return uq.astype(BF16), uk.astype(BF16), uvt.astype(BF16)


def _tile(n, want):
    t = min(n, want)
    assert n % t == 0, (n, t)
    return t


def kernel(x, p, positions, norm_gains, ffn_w_gate, ffn_w_up, ffn_w_down, w_in, w_out, mix_norm_gain, ssm_a_re, ssm_a_im, ssm_b_re, ssm_b_im, ssm_c_re, ssm_c_im, ssm_d, ssm_log_dt, ssm_w_glu, ssm_b_glu, mlstm_conv_w, mlstm_conv_b, mlstm_b_i, mlstm_b_f, mla_q_norm, mla_kv_norm, mla_w_uq, mla_w_ukv, ple_w_proj, ple_w_gate):
    bsz, seq, d = x.shape
    n = bsz * seq
    depth = p.shape[0]
    assert d == D_MODEL and seq % CHUNK == 0

    tm_ffn = _tile(n, 1024)
    tf_ffn = _tile(D_FF, 512)
    tm_proj = _tile(n, 256)
    tm_row = _tile(n, 512)
    t_seq = _tile(seq, 512)
    t_s5 = _tile(seq, 512)
    t_att = _tile(seq, 2048)

    rope = _rope_table(positions, _tile(n, 2048))
    h = x.reshape(n, d).astype(F32)
    gains = norm_gains.astype(F32).reshape(depth, 8, 1, d)
    mixg = mix_norm_gain.astype(F32).reshape(depth, 4, 1, 512)
    wg_all, wu_all, wd_all = ffn_w_gate.astype(BF16), ffn_w_up.astype(BF16), ffn_w_down.astype(BF16)

    for i in range(depth):
        g = gains[i]
        h = _ffn(h, g[0], g[1], wg_all[i, 0], wu_all[i, 0], wd_all[i, 0], tm_ffn, tf_ffn)

        w_main, w_gt = _permute_w_in(w_in[i])
        z, slab, gt = _inproj(h, g[2], w_main, w_gt, tm_proj)

        wb, wba, tabs, wc = _s5_tables(ssm_a_re[i], ssm_a_im[i], ssm_b_re[i], ssm_b_im[i],
                                       ssm_c_re[i], ssm_c_im[i], ssm_log_dt[i])
        out_a = _s5(z, wb, wba, tabs, wc, ssm_d[i].astype(F32).reshape(1, -1), ssm_w_glu[i].astype(BF16),
                    ssm_b_glu[i].astype(F32).reshape(1, -1), mixg[i, 0], bsz, seq, t_s5)
        out_b = _mlstm(z, slab, gt, mlstm_conv_w[i], mlstm_conv_b[i], mlstm_b_i[i], mlstm_b_f[i],
                       mixg[i, 1], bsz, seq, t_seq)
        wuq, wuk, wuvt = _permute_mla(mla_w_uq[i], mla_w_ukv[i])
        q, k, vt = _mla_proj(z, slab, rope, mla_q_norm[i].astype(F32).reshape(1, -1),
                             mla_kv_norm[i].astype(F32).reshape(1, -1), wuq, wuk, wuvt, bsz, seq, t_seq)
        out_c = _flash(q, k, vt, mixg[i, 2], t_att)
        out_d = _retention(z, rope, mixg[i, 3], bsz, seq, t_seq)

        h = _outproj(h, (out_a, out_b, out_c, out_d), w_out[i].astype(BF16), g[3], tm_row)
        h = _ffn(h, g[4], g[5], wg_all[i, 1], wu_all[i, 1], wd_all[i, 1], tm_ffn, tf_ffn)
        h = _ple(h, p[i].reshape(n, PLE_DIM), g[6], g[7], ple_w_gate[i].astype(BF16),
                 ple_w_proj[i].astype(BF16), tm_row)
    return h.reshape(bsz, seq, d).astype(x.dtype)
```

```python
import functools
import math

import jax
import jax.numpy as jnp
import numpy as np
from jax import lax
from jax.experimental import pallas as pl
from jax.experimental.pallas import tpu as pltpu

F32 = jnp.float32
BF16 = jnp.bfloat16

EPS = 1e-6
ROPE_BASE = 10000.0

D_MODEL = 2048
PLE_DIM = 256
D_FF = 5632

SSM_WIDTH = 512
SSM_GROUP = 16
SSM_GROUPS = SSM_WIDTH // SSM_GROUP
SSM_STATE = 64
SSM_LANES = SSM_GROUPS * SSM_STATE

HEADS = 4
QK = 64
HV = 128
CHUNK = 256
MLSTM_CONV = 4
MLA_Q_RANK = 384
MLA_KV_RANK = 256
MLA_NOPE = 128
MLA_QK = MLA_NOPE + QK

V7X_LANES = 128
V7X_SUBLANES = 8
V7X_VMEM_BYTES = 64 * 1024 * 1024
MIB = 1024 * 1024
ROW_PARTS = 2

Z_SU, Z_MV, Z_MO, Z_RV, Z_RG = 0, 512, 1024, 1536, 2048
Z_MQ, Z_MK, Z_CKV, Z_RQ, Z_RK = 2560, 2816, 3072, 3328, 3584
Z_CQ = 3840
Z_WIDTH = 4224
SLAB_WIDTH = 128
SLAB_I, SLAB_F = 64, 68


def _params(sem, vmem_mib):
    return pltpu.CompilerParams(dimension_semantics=sem, vmem_limit_bytes=int(vmem_mib * MIB))


def _rms(x, gain):
    ms = jnp.mean(x * x, axis=-1, keepdims=True)
    return x * lax.rsqrt(ms + EPS) * gain


def _dot(a, b):
    return jnp.dot(a, b, preferred_element_type=F32)


def _dot_nt(a, b):
    return lax.dot_general(a, b, (((1,), (1,)), ((), ())), preferred_element_type=F32)


def _dot_tn(a, b):
    return lax.dot_general(a, b, (((0,), (0,)), ((), ())), preferred_element_type=F32)


def _split_dot(tri, x, nt):
    hi = x.astype(BF16)
    lo = (x - hi.astype(F32)).astype(BF16)
    if nt:
        return _dot(hi, tri) + _dot(lo, tri)
    return _dot(tri, hi) + _dot(tri, lo)


def _rope_kernel(pos_ref, inv_ref, o_ref):
    ang = pos_ref[...].astype(F32) * inv_ref[...]
    lane = lax.broadcasted_iota(jnp.int32, ang.shape, 1)
    sgn = jnp.where(lane < 96, -1.0, 1.0).astype(F32)
    o_ref[...] = jnp.where(lane < 64, jnp.cos(ang), sgn * jnp.sin(ang))


def _rope_table(positions, tm):
    n = positions.size
    inv = 1.0 / (ROPE_BASE ** (jnp.arange(0, QK, 2, dtype=F32) / QK))
    inv = jnp.tile(inv, 4).reshape(1, 128)
    return pl.pallas_call(
        _rope_kernel,
        grid=(n // tm,),
        in_specs=[pl.BlockSpec((tm, 1), lambda i: (i, 0)),
                  pl.BlockSpec((1, 128), lambda i: (0, 0))],
        out_specs=pl.BlockSpec((tm, 128), lambda i: (i, 0)),
        out_shape=jax.ShapeDtypeStruct((n, 128), F32),
        compiler_params=_params(("parallel",), 32),
        name="rope_table",
    )(positions.reshape(n, 1), inv)


def _apply_rope(x, rope, nheads):
    width = nheads * QK
    cos = rope[:, :QK]
    sin = rope[:, QK:]
    if nheads > 1:
        cos = jnp.concatenate([cos] * nheads, axis=-1)
        sin = jnp.concatenate([sin] * nheads, axis=-1)
    if width >= V7X_LANES:
        lane = lax.broadcasted_iota(jnp.int32, x.shape, 1)
        first = (lane % QK) < (QK // 2)
        rot = jnp.where(first, pltpu.roll(x, width - QK // 2, 1), pltpu.roll(x, QK // 2, 1))
    else:
        rot = jnp.concatenate([x[:, QK // 2:], x[:, :QK // 2]], axis=-1)
    return x * cos + rot * sin


def _ffn_kernel(h_ref, gpre_ref, gpost_ref, wg_ref, wu_ref, wd_ref, o_ref, xn_ref):
    j = pl.program_id(1)

    def swiglu_tile(x):
        g = _dot(x, wg_ref[...])
        u = _dot(x, wu_ref[...])
        a = (g * jax.nn.sigmoid(g) * u).astype(BF16)
        return _dot(a, wd_ref[...])

    @pl.when(j == 0)
    def _():
        x = _rms(h_ref[...], gpre_ref[...]).astype(BF16)
        xn_ref[...] = x
        o_ref[...] = swiglu_tile(x)

    last = pl.num_programs(1) - 1

    @pl.when((j > 0) & (j < last))
    def _():
        o_ref[...] += swiglu_tile(xn_ref[...])

    @pl.when(j == last)
    def _():
        y = o_ref[...] + swiglu_tile(xn_ref[...])
        o_ref[...] = h_ref[...] + _rms(y, gpost_ref[...])


def _ffn(h, gpre, gpost, wg, wu, wd, tm, tf):
    n, d = h.shape
    f = wg.shape[1]
    vmem = (4 * tm * d * 4 + tm * d * 2 + 6 * d * tf * 2 + 6 * tm * tf * 4) / MIB + 4
    return pl.pallas_call(
        _ffn_kernel,
        grid=(n // tm, f // tf),
        in_specs=[pl.BlockSpec((tm, d), lambda i, j: (i, 0), pipeline_mode=pl.Buffered(1)),
                  pl.BlockSpec((1, d), lambda i, j: (0, 0)),
                  pl.BlockSpec((1, d), lambda i, j: (0, 0)),
                  pl.BlockSpec((d, tf), lambda i, j: (0, j)),
                  pl.BlockSpec((d, tf), lambda i, j: (0, j)),
                  pl.BlockSpec((tf, d), lambda i, j: (j, 0))],
        out_specs=pl.BlockSpec((tm, d), lambda i, j: (i, 0)),
        out_shape=jax.ShapeDtypeStruct((n, d), F32),
        scratch_shapes=[pltpu.VMEM((tm, d), BF16)],
        compiler_params=_params(("parallel", "arbitrary"), vmem),
        name="ffn",
    )(h, gpre, 0.5 * gpost, wg, wu, wd)


def _inproj_kernel(h_ref, g_ref, w_ref, wgt_ref, z_ref, slab_ref, gt_ref):
    def finish(rows, z):
        z_ref[rows, :] = z[:, :Z_WIDTH].astype(BF16)
        slab_ref[rows, :] = z[:, Z_WIDTH:]

    pending = None
    for rows in _row_parts(h_ref.shape[0], ROW_PARTS):
        xn = _rms(h_ref[rows, :], g_ref[...]).astype(BF16)
        z = _dot(xn, w_ref[...])
        gt_ref[:, rows] = _dot_nt(wgt_ref[...], xn)
        if pending is not None:
            finish(*pending)
        pending = (rows, z)
    finish(*pending)


def _inproj(h, gain, w, wgt, tm):
    n, d = h.shape
    wtot = Z_WIDTH + SLAB_WIDTH
    vmem = (2 * tm * d * 4 + d * wtot * 2 + 2 * tm * wtot * 4 + tm * wtot * 4) / MIB + 6
    return pl.pallas_call(
        _inproj_kernel,
        grid=(n // tm,),
        in_specs=[pl.BlockSpec((tm, d), lambda i: (i, 0)),
                  pl.BlockSpec((1, d), lambda i: (0, 0)),
                  pl.BlockSpec((d, wtot), lambda i: (0, 0), pipeline_mode=pl.Buffered(1)),
                  pl.BlockSpec((8, d), lambda i: (0, 0))],
        out_specs=[pl.BlockSpec((tm, Z_WIDTH), lambda i: (i, 0)),
                   pl.BlockSpec((tm, SLAB_WIDTH), lambda i: (i, 0)),
                   pl.BlockSpec((8, tm), lambda i: (0, i))],
        out_shape=[jax.ShapeDtypeStruct((n, Z_WIDTH), BF16),
                   jax.ShapeDtypeStruct((n, SLAB_WIDTH), F32),
                   jax.ShapeDtypeStruct((8, n), F32)],
        compiler_params=_params(("parallel",), vmem),
        name="inproj",
    )(h, gain, w, wgt)


SCAN_COLS = 512
S5_HALVES = 2
S5_SUB = 256


def _s5_kernel(u_ref, wb_ref, wba_ref, tab_ref, wc_ref, d_ref, wglu_ref, bglu_ref, gain_ref,
               o_ref, x_ref, carry_ref):
    t = u_ref.shape[0]
    hl = SSM_LANES // S5_HALVES
    hw = SSM_WIDTH // S5_HALVES

    @pl.when(pl.program_id(1) == 0)
    def _():
        carry_ref[...] = jnp.zeros_like(carry_ref)

    lanes = []
    for c0 in range(0, SSM_LANES, SCAN_COLS):
        k, off = divmod(c0, hl)
        lanes.append((slice(2 * hl * k + off, 2 * hl * k + off + SCAN_COLS),
                      slice(2 * hl * k + hl + off, 2 * hl * k + hl + off + SCAN_COLS),
                      slice(c0, c0 + SCAN_COLS)))
    last = V7X_SUBLANES - 1

    def project_in(rows):
        u = u_ref[rows, :]
        uf = u.astype(F32)
        first = lax.broadcasted_iota(jnp.int32, uf.shape, 0) % V7X_SUBLANES == 0
        prev = jnp.where(first, 0.0, pltpu.roll(uf, 1, 0)).astype(BF16)
        for k in range(S5_HALVES):
            cols = slice(hw * k, hw * (k + 1))
            x_ref[rows, 2 * hl * k:2 * hl * (k + 1)] = _dot(u[:, cols], wb_ref[k]) + _dot(prev[:, cols], wba_ref[k])

    def scan(row0, nrows):
        for re, im, tab in lanes:
            cr, ci = carry_ref[:, re], carry_ref[:, im]
            for r0 in range(row0, row0 + nrows, V7X_SUBLANES):
                rows = slice(r0, r0 + V7X_SUBLANES)
                xr = x_ref[rows, re]
                xi = x_ref[rows, im]
                for j, shift in ((1, 2), (2, 4)):
                    ar, ai = tab_ref[2 * j, :, tab], tab_ref[2 * j + 1, :, tab]
                    rr = pltpu.roll(xr, shift, 0)
                    ri = pltpu.roll(xi, shift, 0)
                    xr, xi = xr + ar * rr - ai * ri, xi + ar * ri + ai * rr
                pr, pi = tab_ref[6, :, tab], tab_ref[7, :, tab]
                xr, xi = xr + pr * cr - pi * ci, xi + pr * ci + pi * cr
                x_ref[rows, re] = xr
                x_ref[rows, im] = xi
                cr, ci = jnp.broadcast_to(xr[last:, :], xr.shape), jnp.broadcast_to(xi[last:, :], xi.shape)
            carry_ref[:, re] = cr
            carry_ref[:, im] = ci

    def project_out(rows):
        y = jnp.concatenate([_dot(x_ref[rows, 2 * hl * k:2 * hl * (k + 1)].astype(BF16), wc_ref[k])
                             for k in range(S5_HALVES)], axis=-1)
        y = y + d_ref[...] * u_ref[rows, :].astype(F32)
        g = jax.nn.gelu(y)
        out = g * jax.nn.sigmoid(_dot(g.astype(BF16), wglu_ref[...]) + bglu_ref[...])
        o_ref[rows, :] = _rms(out, gain_ref[...]).astype(o_ref.dtype)

    sub = min(S5_SUB, t)
    for r0 in range(0, t, sub):
        project_in(slice(r0, r0 + sub))
    for r0 in range(0, t, sub):
        scan(r0, sub)
        project_out(slice(r0, r0 + sub))


def _s5(z, wb, wba, tabs, wc, d_skip, wglu, bglu, gain, bsz, seq, t):
    n = bsz * seq
    nt = seq // t
    nl2 = 2 * SSM_LANES
    const = lambda b, i: (0, 0)
    vmem = (t * nl2 * 4 * 3 + 2 * SSM_WIDTH * nl2 * 2 * 2 + 8 * 8 * SSM_LANES * 4 * 2) / MIB + 8
    return pl.pallas_call(
        _s5_kernel,
        grid=(bsz, nt),
        in_specs=[pl.BlockSpec((t, SSM_WIDTH), lambda b, i: (b * nt + i, Z_SU // SSM_WIDTH)),
                  pl.BlockSpec((S5_HALVES, SSM_WIDTH // S5_HALVES, nl2 // S5_HALVES), lambda b, i: (0, 0, 0)),
                  pl.BlockSpec((S5_HALVES, SSM_WIDTH // S5_HALVES, nl2 // S5_HALVES), lambda b, i: (0, 0, 0)),
                  pl.BlockSpec((8, V7X_SUBLANES, SSM_LANES), lambda b, i: (0, 0, 0)),
                  pl.BlockSpec((S5_HALVES, nl2 // S5_HALVES, SSM_WIDTH // S5_HALVES), lambda b, i: (0, 0, 0)),
                  pl.BlockSpec((1, SSM_WIDTH), const),
                  pl.BlockSpec((SSM_WIDTH, SSM_WIDTH), const),
                  pl.BlockSpec((1, SSM_WIDTH), const),
                  pl.BlockSpec((1, SSM_WIDTH), const)],
        out_specs=pl.BlockSpec((t, SSM_WIDTH), lambda b, i: (b * nt + i, 0)),
        out_shape=jax.ShapeDtypeStruct((n, SSM_WIDTH), BF16),
        scratch_shapes=[pltpu.VMEM((t, nl2), F32), pltpu.VMEM((V7X_SUBLANES, nl2), F32)],
        compiler_params=_params(("arbitrary", "arbitrary"), vmem),
        name="s5",
    )(z, wb, wba, tabs, wc, d_skip, wglu, bglu, gain)


def _s5_tables(a_re, a_im, b_re, b_im, c_re, c_im, log_dt):
    lr, li = a_re.astype(F32), a_im.astype(F32)
    dt = jnp.exp(log_dt.astype(F32))[:, None]
    mag = jnp.exp(lr * dt)
    ar, ai = mag * jnp.cos(li * dt), mag * jnp.sin(li * dt)
    den = lr * lr + li * li
    nr, ni = ar - 1.0, ai
    zr = (nr * lr + ni * li) / den
    zi = (ni * lr - nr * li) / den
    br_, bi_ = b_re.astype(F32), b_im.astype(F32)
    bbr = zr[..., None] * br_ - zi[..., None] * bi_
    bbi = zr[..., None] * bi_ + zi[..., None] * br_
    eye = jnp.eye(SSM_GROUPS, dtype=F32)
    wb_re = jnp.einsum('gnc,gh->gchn', bbr, eye).reshape(SSM_WIDTH, SSM_LANES)
    wb_im = jnp.einsum('gnc,gh->gchn', bbi, eye).reshape(SSM_WIDTH, SSM_LANES)
    hw, hl = SSM_WIDTH // S5_HALVES, SSM_LANES // S5_HALVES

    def halves_in(w_re, w_im):
        return jnp.stack([jnp.concatenate([w[hw * k:hw * (k + 1), hl * k:hl * (k + 1)] for w in (w_re, w_im)], axis=1)
                          for k in range(S5_HALVES)]).astype(BF16)

    ar_l, ai_l = ar.reshape(1, SSM_LANES), ai.reshape(1, SSM_LANES)
    wb = halves_in(wb_re, wb_im)
    wba = halves_in(wb_re * ar_l - wb_im * ai_l, wb_re * ai_l + wb_im * ar_l)
    wc_re = jnp.einsum('gcn,gh->gnhc', c_re.astype(F32), eye).reshape(SSM_LANES, SSM_WIDTH)
    wc_im = jnp.einsum('gcn,gh->gnhc', c_im.astype(F32), eye).reshape(SSM_LANES, SSM_WIDTH)
    wc = jnp.stack([jnp.concatenate([w[hl * k:hl * (k + 1), hw * k:hw * (k + 1)] for w in (wc_re, -wc_im)], axis=0)
                    for k in range(S5_HALVES)]).astype(BF16)

    a1 = (ar.reshape(1, SSM_LANES), ai.reshape(1, SSM_LANES))

    def cmul(p, q):
        return (p[0] * q[0] - p[1] * q[1], p[0] * q[1] + p[1] * q[0])

    pows = [a1]
    for _ in range(7):
        pows.append(cmul(pows[-1], a1))
    row = jnp.arange(V7X_SUBLANES)[:, None]
    tabs = []
    for shift in (1, 2, 4):
        pr, pi = pows[shift - 1]
        keep = row >= shift
        tabs += [jnp.where(keep, pr, 0.0), jnp.where(keep, pi, 0.0)]
    tabs += [jnp.concatenate([p[0] for p in pows], axis=0), jnp.concatenate([p[1] for p in pows], axis=0)]
    return wb, wba, jnp.stack(tabs).astype(F32), wc


def _ret_kernel(q_ref, k_ref, v_ref, g_ref, rope_ref, dmask_ref, qdec_ref, kdec_ref, cdec_ref, gain_ref,
                o_ref, state_ref):
    t = q_ref.shape[0]

    @pl.when(pl.program_id(1) == 0)
    def _():
        state_ref[...] = jnp.zeros_like(state_ref)

    for c in range(t // CHUNK):
        rows = slice(c * CHUNK, (c + 1) * CHUNK)
        rope = rope_ref[rows, :]
        q = _apply_rope(q_ref[rows, :].astype(F32), rope, HEADS)
        k = _apply_rope(k_ref[rows, :].astype(F32), rope, HEADS) * (QK ** -0.5)
        for h in range(HEADS):
            qh = q[:, h * QK:(h + 1) * QK].astype(BF16)
            kh = k[:, h * QK:(h + 1) * QK]
            vh = v_ref[rows, h * HV:(h + 1) * HV]
            s = _dot_nt(qh, kh.astype(BF16)) * dmask_ref[h]
            inner = _dot(s.astype(BF16), vh)
            r_old = state_ref[h]
            cross = _dot(qh, r_old.astype(BF16)) * qdec_ref[h]
            state_ref[h] = cdec_ref[h] * r_old + _dot_tn((kh * kdec_ref[h]).astype(BF16), vh)
            y = _rms(inner + cross, gain_ref[:, h * HV:(h + 1) * HV])
            gate = g_ref[rows, h * HV:(h + 1) * HV].astype(F32)
            o_ref[rows, h * HV:(h + 1) * HV] = (gate * jax.nn.sigmoid(gate) * y).astype(o_ref.dtype)


def _ret_tables():
    log_g = jnp.log1p(-jnp.exp2(-5.0 - jnp.arange(HEADS, dtype=F32)))
    idx = jnp.arange(CHUNK)
    diff = idx[:, None] - idx[None, :]
    causal = diff >= 0
    dmask = jnp.where(causal, jnp.exp(jnp.where(causal, diff, 0).astype(F32) * log_g[:, None, None]), 0.0)
    q_decay = jnp.exp((idx + 1).astype(F32)[None, :] * log_g[:, None])
    k_decay = jnp.exp((CHUNK - 1 - idx).astype(F32)[None, :] * log_g[:, None])
    chunk_decay = jnp.exp(CHUNK * log_g)
    qdec = jnp.broadcast_to(q_decay[:, :, None], (HEADS, CHUNK, HV))
    kdec = jnp.broadcast_to(k_decay[:, :, None], (HEADS, CHUNK, QK))
    cdec = jnp.broadcast_to(chunk_decay[:, None, None], (HEADS, QK, HV))
    return dmask, qdec, kdec, cdec


def _retention(z, rope, gain, bsz, seq, t):
    n = bsz * seq
    nt = seq // t
    dmask, qdec, kdec, cdec = _ret_tables()
    c3 = lambda b, i: (0, 0, 0)

    def col(width, lane0):
        return pl.BlockSpec((t, width), lambda b, i: (b * nt + i, lane0 // width))

    return pl.pallas_call(
        _ret_kernel,
        grid=(bsz, nt),
        in_specs=[col(256, Z_RQ), col(256, Z_RK), col(512, Z_RV), col(512, Z_RG),
                  pl.BlockSpec((t, 128), lambda b, i: (b * nt + i, 0)),
                  pl.BlockSpec((HEADS, CHUNK, CHUNK), c3),
                  pl.BlockSpec((HEADS, CHUNK, HV), c3),
                  pl.BlockSpec((HEADS, CHUNK, QK), c3),
                  pl.BlockSpec((HEADS, QK, HV), c3),
                  pl.BlockSpec((1, HEADS * HV), lambda b, i: (0, 0))],
        out_specs=pl.BlockSpec((t, HEADS * HV), lambda b, i: (b * nt + i, 0)),
        out_shape=jax.ShapeDtypeStruct((n, HEADS * HV), BF16),
        scratch_shapes=[pltpu.VMEM((HEADS, QK, HV), F32)],
        compiler_params=_params(("arbitrary", "arbitrary"), 32),
        name="retention",
    )(z, z, z, z, rope, dmask, qdec, kdec, cdec, gain)


TAIL = V7X_SUBLANES


def _mlstm_kernel(q_ref, k_ref, v_ref, og_ref, slab_ref, gt_ref, cw_ref, cb_ref, brow_ref, bcol_ref,
                  tril_ref, triu_ref, gain_ref, o_ref, xbuf_ref, c_ref, m_ref):
    t = q_ref.shape[0]
    width = 2 * HEADS * QK

    @pl.when(pl.program_id(1) == 0)
    def _():
        xbuf_ref[0:TAIL, :] = jnp.zeros((TAIL, width), F32)
        c_ref[...] = jnp.zeros_like(c_ref)
        m_ref[...] = jnp.zeros_like(m_ref)

    xbuf_ref[TAIL:, 0:HEADS * QK] = q_ref[...].astype(F32)
    xbuf_ref[TAIL:, HEADS * QK:] = k_ref[...].astype(F32)
    acc = cb_ref[...] + cw_ref[MLSTM_CONV - 1:MLSTM_CONV, :] * xbuf_ref[TAIL:, :]
    for j in range(MLSTM_CONV - 1):
        off = TAIL - (MLSTM_CONV - 1) + j
        acc = acc + cw_ref[j:j + 1, :] * xbuf_ref[off:off + t, :]
    xbuf_ref[0:TAIL, :] = xbuf_ref[t:t + TAIL, :]
    qk = acc * jax.nn.sigmoid(acc)

    gr = gt_ref[...] + brow_ref[...]
    is_f_row = lax.broadcasted_iota(jnp.int32, gr.shape, 0) >= HEADS
    lf_r = jnp.where(is_f_row, jax.nn.log_sigmoid(gr), 0.0)
    gc = slab_ref[...] + bcol_ref[...]
    lane = lax.broadcasted_iota(jnp.int32, gc.shape, 1)
    is_f_col = (lane >= SLAB_F) & (lane < SLAB_F + HEADS)
    lf_c = jnp.where(is_f_col, jax.nn.log_sigmoid(gc), 0.0)

    row = lax.broadcasted_iota(jnp.int32, (CHUNK, CHUNK), 0)
    colid = lax.broadcasted_iota(jnp.int32, (CHUNK, CHUNK), 1)
    causal = colid <= row
    ones_col = (lax.broadcasted_iota(jnp.int32, (CHUNK, HV), 1) == 0).astype(BF16)

    for c in range(t // CHUNK):
        rows = slice(c * CHUNK, (c + 1) * CHUNK)
        bh_r = _split_dot(triu_ref[...], lf_r[:, rows], nt=True)
        bh_c = _split_dot(tril_ref[...], lf_c[rows, :], nt=False)
        a_r = gr[0:HEADS, rows] - bh_r[HEADS:, :]
        a_c = gc[rows, SLAB_I:SLAB_I + HEADS] - bh_c[:, SLAB_F:SLAB_F + HEADS]
        bh_c = bh_c[:, SLAB_F:SLAB_F + HEADS]
        hs = range(HEADS)
        qh = [(qk[rows, h * QK:(h + 1) * QK] * (QK ** -0.5)).astype(BF16) for h in hs]
        kh = [qk[rows, (HEADS + h) * QK:(HEADS + h + 1) * QK] for h in hs]
        vaug = [jnp.concatenate([v_ref[rows, h * HV:(h + 1) * HV], ones_col], axis=-1) for h in hs]
        m_old = [m_ref[h][0:1, 0:1] for h in hs]
        c_old = [c_ref[h] for h in hs]
        qk_raw = [_dot_nt(qh[h], kh[h].astype(BF16)) for h in hs]
        q_c = [_dot(qh[h], c_old[h].astype(BF16)) for h in hs]
        amat = [jnp.where(causal, a_r[h:h + 1, :], -jnp.inf) for h in hs]
        big_m = [jnp.maximum(jnp.max(amat[h], axis=-1, keepdims=True), m_old[h]) for h in hs]
        s = [(qk_raw[h] * jnp.exp(amat[h] - big_m[h])).astype(BF16) for h in hs]
        sc = [jnp.exp(m_old[h] - big_m[h]) for h in hs]
        tot = [_dot(s[h], vaug[h]) + sc[h] * q_c[h] for h in hs]
        m_last = [big_m[h][CHUNK - 1:CHUNK, :] for h in hs]
        wk = [jnp.exp(a_c[:, h:h + 1] - m_last[h]) for h in hs]
        upd = [_dot_tn((kh[h] * wk[h]).astype(BF16), vaug[h]) for h in hs]
        for h in hs:
            c_ref[h] = jnp.exp(m_old[h] - m_last[h]) * c_old[h] + upd[h]
            m_ref[h] = jnp.broadcast_to(bh_c[CHUNK - 1:CHUNK, h:h + 1] + m_last[h], m_ref.shape[1:])
        for h in hs:
            den = jnp.maximum(jnp.abs(tot[h][:, HV:HV + 1]), jnp.exp(-(bh_c[:, h:h + 1] + big_m[h])))
            y = _rms(tot[h][:, :HV] / den, gain_ref[:, h * HV:(h + 1) * HV])
            og = og_ref[rows, h * HV:(h + 1) * HV].astype(F32)
            o_ref[rows, h * HV:(h + 1) * HV] = (jax.nn.sigmoid(og) * y).astype(o_ref.dtype)


def _mlstm(z, slab, gt, conv_w, conv_b, b_i, b_f, gain, bsz, seq, t):
    n = bsz * seq
    nt = seq // t
    idx = jnp.arange(CHUNK)
    tril = (idx[None, :] <= idx[:, None]).astype(BF16)
    triu = tril.T
    brow = jnp.concatenate([b_i, b_f]).astype(F32).reshape(2 * HEADS, 1)
    bcol = jnp.zeros((1, SLAB_WIDTH), F32).at[0, SLAB_I:SLAB_I + 2 * HEADS].set(brow[:, 0])
    c2 = lambda b, i: (0, 0)

    def col(width, lane0):
        return pl.BlockSpec((t, width), lambda b, i: (b * nt + i, lane0 // width))

    return pl.pallas_call(
        _mlstm_kernel,
        grid=(bsz, nt),
        in_specs=[col(256, Z_MQ), col(256, Z_MK), col(512, Z_MV), col(512, Z_MO),
                  pl.BlockSpec((t, SLAB_WIDTH), lambda b, i: (b * nt + i, 0)),
                  pl.BlockSpec((2 * HEADS, t), lambda b, i: (0, b * nt + i)),
                  pl.BlockSpec((MLSTM_CONV, 2 * HEADS * QK), c2),
                  pl.BlockSpec((1, 2 * HEADS * QK), c2),
                  pl.BlockSpec((2 * HEADS, 1), c2),
                  pl.BlockSpec((1, SLAB_WIDTH), c2),
                  pl.BlockSpec((CHUNK, CHUNK), c2),
                  pl.BlockSpec((CHUNK, CHUNK), c2),
                  pl.BlockSpec((1, HEADS * HV), c2)],
        out_specs=pl.BlockSpec((t, HEADS * HV), lambda b, i: (b * nt + i, 0)),
        out_shape=jax.ShapeDtypeStruct((n, HEADS * HV), BF16),
        scratch_shapes=[pltpu.VMEM((t + TAIL, 2 * HEADS * QK), F32),
                        pltpu.VMEM((HEADS, QK, 2 * HV), F32),
                        pltpu.VMEM((HEADS, V7X_SUBLANES, V7X_LANES), F32)],
        compiler_params=_params(("arbitrary", "arbitrary"), 32),
        name="mlstm",
    )(z, z, z, z, slab, gt, conv_w.astype(F32), conv_b.astype(F32).reshape(1, -1), brow, bcol, tril, triu, gain)


VT_ROWS = HV + 16


def _mla_proj_kernel(cq_ref, ckv_ref, slab_ref, rope_ref, qn_ref, kvn_ref, wuq_ref, wuk_ref, wuvt_ref,
                     q_ref, k_ref, vt_ref):
    rope = rope_ref[...]
    cq = _rms(cq_ref[...].astype(F32), qn_ref[...]).astype(BF16)
    ckv = _rms(ckv_ref[...].astype(F32), kvn_ref[...]).astype(BF16)
    q = _dot(cq, wuq_ref[...])
    kn = _dot(ckv, wuk_ref[...])
    vt = _dot_nt(wuvt_ref[...], ckv)
    scale = (MLA_QK ** -0.5) * math.log2(math.e)
    q_rope = _apply_rope(q[:, HEADS * MLA_NOPE:], rope, HEADS)
    k_rope = _apply_rope(slab_ref[:, 0:QK], rope, 1)
    pad = VT_ROWS - HV
    ones_row = (lax.broadcasted_iota(jnp.int32, (pad, vt.shape[1]), 0) == 0).astype(BF16)
    for h in range(HEADS):
        q_ref[0, h, :, 0:MLA_NOPE] = (q[:, h * MLA_NOPE:(h + 1) * MLA_NOPE] * scale).astype(BF16)
        q_ref[0, h, :, MLA_NOPE:] = (q_rope[:, h * QK:(h + 1) * QK] * scale).astype(BF16)
        k_ref[0, h, :, 0:MLA_NOPE] = kn[:, h * MLA_NOPE:(h + 1) * MLA_NOPE].astype(BF16)
        k_ref[0, h, :, MLA_NOPE:] = k_rope.astype(BF16)
        vt_ref[0, h, 0:HV, :] = vt[h * HV:(h + 1) * HV, :].astype(BF16)
        vt_ref[0, h, HV:, :] = ones_row


def _mla_proj(z, slab, rope, q_norm, kv_norm, wuq, wuk, wuvt, bsz, seq, t):
    nt = seq // t
    c2 = lambda b, i: (0, 0)
    return pl.pallas_call(
        _mla_proj_kernel,
        grid=(bsz, nt),
        in_specs=[pl.BlockSpec((t, MLA_Q_RANK), lambda b, i: (b * nt + i, Z_CQ // MLA_Q_RANK)),
                  pl.BlockSpec((t, MLA_KV_RANK), lambda b, i: (b * nt + i, Z_CKV // MLA_KV_RANK)),
                  pl.BlockSpec((t, SLAB_WIDTH), lambda b, i: (b * nt + i, 0)),
                  pl.BlockSpec((t, 128), lambda b, i: (b * nt + i, 0)),
                  pl.BlockSpec((1, MLA_Q_RANK), c2),
                  pl.BlockSpec((1, MLA_KV_RANK), c2),
                  pl.BlockSpec((MLA_Q_RANK, HEADS * MLA_QK), c2),
                  pl.BlockSpec((MLA_KV_RANK, HEADS * MLA_NOPE), c2),
                  pl.BlockSpec((HEADS * HV, MLA_KV_RANK), c2)],
        out_specs=[pl.BlockSpec((1, HEADS, t, MLA_QK), lambda b, i: (b, 0, i, 0)),
                   pl.BlockSpec((1, HEADS, t, MLA_QK), lambda b, i: (b, 0, i, 0)),
                   pl.BlockSpec((1, HEADS, VT_ROWS, t), lambda b, i: (b, 0, 0, i))],
        out_shape=[jax.ShapeDtypeStruct((bsz, HEADS, seq, MLA_QK), BF16),
                   jax.ShapeDtypeStruct((bsz, HEADS, seq, MLA_QK), BF16),
                   jax.ShapeDtypeStruct((bsz, HEADS, VT_ROWS, seq), BF16)],
        compiler_params=_params(("parallel", "parallel"), 40),
        name="mla_proj",
    )(z, z, slab, rope, q_norm, kv_norm, wuq, wuk, wuvt)


FLASH_SUB = 1024


def _flash_kernel(q_ref, k_ref, vt_ref, gain_ref, o_ref, m_ref, acc_ref):
    qi = pl.program_id(1)
    ki = pl.program_id(2)
    tk = k_ref.shape[2]
    sub = min(FLASH_SUB, tk)

    @pl.when(ki == 0)
    def _():
        m_ref[...] = jnp.full_like(m_ref, -jnp.inf)
        acc_ref[...] = jnp.zeros_like(acc_ref)

    def step(diagonal):
        chains = [(h, c) for h in range(HEADS) for c in range(tk // sub)]

        def scores(h, c):
            c0 = c * sub if diagonal else 0
            st = _dot_nt(k_ref[0, h, c * sub:(c + 1) * sub, :], q_ref[0, h, c0:, :])
            if diagonal:
                key = c * sub + lax.broadcasted_iota(jnp.int32, st.shape, 0)
                qry = c0 + lax.broadcasted_iota(jnp.int32, st.shape, 1)
                st = jnp.where(key <= qry, st, -jnp.inf)
            return st

        st_next = scores(*chains[0])
        for idx, (h, c) in enumerate(chains):
            st = st_next
            if idx + 1 < len(chains):
                st_next = scores(*chains[idx + 1])
            c0 = c * sub if diagonal else 0
            m_old = m_ref[h, 0:1, c0:]
            m_new = jnp.maximum(m_old, jnp.max(st, axis=0, keepdims=True))
            alpha = jnp.exp2(m_old - m_new)
            pt = jnp.exp2(st - m_new).astype(BF16)
            acc_ref[h, :, c0:] = alpha * acc_ref[h, :, c0:] + _dot(vt_ref[0, h, :, c * sub:(c + 1) * sub], pt)
            m_ref[h, :, c0:] = jnp.broadcast_to(m_new, (V7X_SUBLANES, m_new.shape[1]))

    @pl.when(ki < qi)
    def _():
        step(False)

    @pl.when(ki == qi)
    def _():
        step(True)
        for h in range(HEADS):
            acc = acc_ref[h]
            yt = acc[0:HV, :] / acc[HV:HV + 1, :]
            yt = yt * lax.rsqrt(jnp.mean(yt * yt, axis=0, keepdims=True) + EPS)
            o_ref[:, h * HV:(h + 1) * HV] = (yt.T * gain_ref[:, h * HV:(h + 1) * HV]).astype(o_ref.dtype)


def _flash(q, k, vt, gain, t):
    bsz, _, seq, _ = q.shape
    nt = seq // t
    return pl.pallas_call(
        _flash_kernel,
        grid=(bsz, nt, nt),
        in_specs=[pl.BlockSpec((1, HEADS, t, MLA_QK), lambda b, i, j: (b, 0, i, 0)),
                  pl.BlockSpec((1, HEADS, t, MLA_QK), lambda b, i, j: (b, 0, jnp.minimum(i, j), 0)),
                  pl.BlockSpec((1, HEADS, VT_ROWS, t), lambda b, i, j: (b, 0, 0, jnp.minimum(i, j))),
                  pl.BlockSpec((1, HEADS * HV), lambda b, i, j: (0, 0))],
        out_specs=pl.BlockSpec((t, HEADS * HV), lambda b, i, j: (b * nt + i, 0)),
        out_shape=jax.ShapeDtypeStruct((bsz * seq, HEADS * HV), BF16),
        scratch_shapes=[pltpu.VMEM((HEADS, V7X_SUBLANES, t), F32),
                        pltpu.VMEM((HEADS, VT_ROWS, t), F32)],
        compiler_params=_params(("parallel", "parallel", "arbitrary"), 56),
        name="mla_flash",
    )(q, k, vt, gain)


def _row_parts(n_rows, parts):
    step = n_rows // parts
    return [slice(i * step, (i + 1) * step) for i in range(parts)]


def _outproj_kernel(h_ref, a_ref, b_ref, c_ref, d_ref, w_ref, g_ref, o_ref):
    ys = []
    for rows in _row_parts(h_ref.shape[0], ROW_PARTS):
        y = _dot(a_ref[rows, :], w_ref[0:512, :])
        y += _dot(b_ref[rows, :], w_ref[512:1024, :])
        y += _dot(c_ref[rows, :], w_ref[1024:1536, :])
        y += _dot(d_ref[rows, :], w_ref[1536:2048, :])
        if ys:
            prev_rows, prev_y = ys.pop()
            o_ref[prev_rows, :] = h_ref[prev_rows, :] + _rms(prev_y, g_ref[...])
        ys.append((rows, y))
    prev_rows, prev_y = ys.pop()
    o_ref[prev_rows, :] = h_ref[prev_rows, :] + _rms(prev_y, g_ref[...])


def _outproj(h, mixes, w, gain, tm):
    n, d = h.shape
    row = lambda i: (i, 0)
    vmem = (4 * tm * d * 4 + d * d * 2 + 8 * tm * 512 * 2 + 2 * tm * d * 4) / MIB + 6
    return pl.pallas_call(
        _outproj_kernel,
        grid=(n // tm,),
        in_specs=[pl.BlockSpec((tm, d), row)] + [pl.BlockSpec((tm, 512), row)] * 4 +
                 [pl.BlockSpec((d, d), lambda i: (0, 0), pipeline_mode=pl.Buffered(1)),
                  pl.BlockSpec((1, d), lambda i: (0, 0))],
        out_specs=pl.BlockSpec((tm, d), row),
        out_shape=jax.ShapeDtypeStruct((n, d), F32),
        compiler_params=_params(("parallel",), vmem),
        name="outproj",
    )(h, *mixes, w, gain)


def _ple_kernel(h_ref, p_ref, g6_ref, g7_ref, wg_ref, wp_ref, o_ref):
    def finish(rows, h, gate_pre, emb_raw):
        o_ref[rows, :] = h + jax.nn.sigmoid(gate_pre) * _rms(emb_raw, g7_ref[...])

    pending = None
    for rows in _row_parts(h_ref.shape[0], ROW_PARTS):
        h = h_ref[rows, :]
        gate_pre = _dot(_rms(h, g6_ref[...]).astype(BF16), wg_ref[...])
        emb_raw = _dot(p_ref[rows, :].astype(BF16), wp_ref[...])
        if pending is not None:
            finish(*pending)
        pending = (rows, h, gate_pre, emb_raw)
    finish(*pending)


def _ple(h, p, g6, g7, wg, wp, tm):
    n, d = h.shape
    row = lambda i: (i, 0)
    c2 = lambda i: (0, 0)
    vmem = (4 * tm * d * 4 + d * d * 2 + 2 * PLE_DIM * d * 2 + 3 * tm * d * 4) / MIB + 6
    return pl.pallas_call(
        _ple_kernel,
        grid=(n // tm,),
        in_specs=[pl.BlockSpec((tm, d), row), pl.BlockSpec((tm, PLE_DIM), row),
                  pl.BlockSpec((1, d), c2), pl.BlockSpec((1, d), c2),
                  pl.BlockSpec((d, d), c2, pipeline_mode=pl.Buffered(1)),
                  pl.BlockSpec((PLE_DIM, d), c2)],
        out_specs=pl.BlockSpec((tm, d), row),
        out_shape=jax.ShapeDtypeStruct((n, d), F32),
        compiler_params=_params(("parallel",), vmem),
        name="ple",
    )(h, p, g6, g7, wg, wp)


def _permute_w_in(w_in):
    sizes = (SSM_WIDTH, 256, 256, 512, 512, 4, 4, MLA_Q_RANK, MLA_KV_RANK, QK, 256, 256, 512, 512)
    offs = np.concatenate([[0], np.cumsum(sizes)])
    (s_u, m_q, m_k, m_v, m_o, m_i, m_f, a_cq, a_ckv, a_kr, r_q, r_k, r_v, r_g) = [
        w_in[:, int(offs[i]):int(offs[i + 1])] for i in range(len(sizes))]
    pad = jnp.zeros((w_in.shape[0], SLAB_WIDTH - QK - 2 * HEADS), w_in.dtype)
    main = jnp.concatenate([s_u, m_v, m_o, r_v, r_g, m_q, m_k, a_ckv, r_q, r_k, a_cq,
                            a_kr, m_i, m_f, pad], axis=1).astype(BF16)
    gates_t = jnp.concatenate([m_i, m_f], axis=1).T.astype(BF16)
    return main, gates_t


def _permute_mla(w_uq, w_ukv):
    uq = w_uq.reshape(MLA_Q_RANK, HEADS, MLA_QK)
    uq = jnp.concatenate([uq[:, :, :MLA_NOPE].reshape(MLA_Q_RANK, -1),
                          uq[:, :, MLA_NOPE:].reshape(MLA_Q_RANK, -1)], axis=1)
    ukv = w_ukv.reshape(MLA_KV_RANK, HEADS, MLA_NOPE + HV)
    uk = ukv[:, :, :MLA_NOPE].reshape(MLA_KV_RANK, -1)
    uvt = ukv[:, :, MLA_NOPE:].reshape(MLA_KV_RANK, -1).T
    return uq.astype(BF16), uk.astype(BF16), uvt.astype(BF16)


def _tile(n, want):
    t = min(n, want)
    assert n % t == 0, (n, t)
    return t


def kernel(x, p, positions, norm_gains, ffn_w_gate, ffn_w_up, ffn_w_down, w_in, w_out, mix_norm_gain, ssm_a_re, ssm_a_im, ssm_b_re, ssm_b_im, ssm_c_re, ssm_c_im, ssm_d, ssm_log_dt, ssm_w_glu, ssm_b_glu, mlstm_conv_w, mlstm_conv_b, mlstm_b_i, mlstm_b_f, mla_q_norm, mla_kv_norm, mla_w_uq, mla_w_ukv, ple_w_proj, ple_w_gate):
    bsz, seq, d = x.shape
    n = bsz * seq
    depth = p.shape[0]
    assert d == D_MODEL and seq % CHUNK == 0

    tm_ffn = _tile(n, 1024)
    tf_ffn = _tile(D_FF, 512)
    tm_proj = _tile(n, 512)
    tm_row = _tile(n, 512)
    t_seq = _tile(seq, 512)
    t_s5 = _tile(seq, 512)
    t_att = _tile(seq, 2048)

    rope = _rope_table(positions, _tile(n, 2048))
    h = x.reshape(n, d).astype(F32)
    gains = norm_gains.astype(F32).reshape(depth, 8, 1, d)
    mixg = mix_norm_gain.astype(F32).reshape(depth, 4, 1, 512)

    def ffn_weights(layer, which):
        return tuple(w[layer, which].astype(BF16) for w in (ffn_w_gate, ffn_w_up, ffn_w_down))

    for i in range(depth):
        g = gains[i]
        h = _ffn(h, g[0], g[1], *ffn_weights(i, 0), tm_ffn, tf_ffn)

        w_main, w_gt = _permute_w_in(w_in[i])
        z, slab, gt = _inproj(h, g[2], w_main, w_gt, tm_proj)

        wb, wba, tabs, wc = _s5_tables(ssm_a_re[i], ssm_a_im[i], ssm_b_re[i], ssm_b_im[i],
                                       ssm_c_re[i], ssm_c_im[i], ssm_log_dt[i])
        out_a = _s5(z, wb, wba, tabs, wc, ssm_d[i].astype(F32).reshape(1, -1), ssm_w_glu[i].astype(BF16),
                    ssm_b_glu[i].astype(F32).reshape(1, -1), mixg[i, 0], bsz, seq, t_s5)
        out_b = _mlstm(z, slab, gt, mlstm_conv_w[i], mlstm_conv_b[i], mlstm_b_i[i], mlstm_b_f[i],
                       mixg[i, 1], bsz, seq, t_seq)
        wuq, wuk, wuvt = _permute_mla(mla_w_uq[i], mla_w_ukv[i])
        q, k, vt = _mla_proj(z, slab, rope, mla_q_norm[i].astype(F32).reshape(1, -1),
                             mla_kv_norm[i].astype(F32).reshape(1, -1), wuq, wuk, wuvt, bsz, seq, t_seq)
        out_c = _flash(q, k, vt, mixg[i, 2], t_att)
        out_d = _retention(z, rope, mixg[i, 3], bsz, seq, t_seq)

        h = _outproj(h, (out_a, out_b, out_c, out_d), w_out[i].astype(BF16), g[3], tm_row)
        h = _ffn(h, g[4], g[5], *ffn_weights(i, 1), tm_ffn, tf_ffn)
        h = _ple(h, p[i].reshape(n, PLE_DIM), g[6], g[7], ple_w_gate[i].astype(BF16),
                 ple_w_proj[i].astype(BF16), tm_row)
    return h.reshape(bsz, seq, d).astype(x.dtype)
```
